```python
import math
import jax, jax.numpy as jnp
from jax import lax
import numpy as np

D_MODEL = 2048
BATCH = 1
SEQ = 8192
DEPTH = 1
DEC_BATCH = 32
DEC_SEQ = 4
PAST_LEN = 16384
PAGE_SIZE = 128

ATTN_WIDTH = D_MODEL // 2
CONV_WIDTH = D_MODEL - ATTN_WIDTH
V_HEAD_DIM = 128
QK_HEAD_DIM = V_HEAD_DIM // 2
N_HEADS = ATTN_WIDTH // V_HEAD_DIM
CONV_K = 3
N_BUCKETS = 32
MAX_DISTANCE = 2048
N_GROUPS = 4
EXPERTS_PER_GROUP = 4
N_EXPERTS = N_GROUPS * EXPERTS_PER_GROUP
TOP_K_INNER = 2
EXPERT_FF = D_MODEL // 4
Q_BLOCK = 128
RMS_EPS = 1e-6
NEG_INF = -1e30
PROJ_WIDTH = 3 * ATTN_WIDTH + 3 * CONV_WIDTH

kernel_name = "hymba_diffattn_shortconv_hmoe_step"


def rms_norm(x, g):
    xf = x.astype(jnp.float32)
    y = xf * lax.rsqrt(jnp.mean(xf * xf, axis=-1, keepdims=True) + RMS_EPS)
    return (y * g.astype(jnp.float32)).astype(x.dtype)


def rel_bucket(q_pos, k_pos):
    n = jnp.maximum(q_pos - k_pos, 0)
    max_exact = N_BUCKETS // 2
    nf = jnp.maximum(n, 1).astype(jnp.float32)
    large = max_exact + (jnp.log(nf / max_exact) / math.log(MAX_DISTANCE / max_exact)
                         * (N_BUCKETS - max_exact)).astype(jnp.int32)
    large = jnp.minimum(large, N_BUCKETS - 1)
    return jnp.where(n < max_exact, n, large)


def rel_bias(table, q_pos, k_pos):
    b = rel_bucket(q_pos[:, None], k_pos[None, :])
    return jnp.moveaxis(table.astype(jnp.float32)[b], -1, 0)


def attn_partial(q, k, v, bias, mask):
    s = jnp.einsum('bhcqd,bkhcd->bhcqk', q, k.astype(jnp.float32)) + bias[None, :, None]
    if mask is not None:
        s = jnp.where(mask, s, NEG_INF)
    m = jnp.max(s, axis=-1)
    p = jnp.exp(s - m[..., None])
    l = jnp.sum(p, axis=-1)
    acc = jnp.einsum('bhcqk,bkhv->bhcqv', p, v.astype(jnp.float32))
    return (m, l, acc)


def merge_stats(a, b):
    m1, l1, acc1 = a
    m2, l2, acc2 = b
    m = jnp.maximum(m1, m2)
    a1 = jnp.exp(m1 - m)
    a2 = jnp.exp(m2 - m)
    return (m, l1 * a1 + l2 * a2, acc1 * a1[..., None] + acc2 * a2[..., None])


def diff_combine(stats, lam, lam_init, g_sub):
    m, l, acc = stats
    o = acc / l[..., None]
    o = o[:, :, 0] - lam * o[:, :, 1]
    o = rms_norm(o, g_sub) * (1.0 - lam_init)
    return jnp.transpose(o, (0, 2, 1, 3))


def project(xn, w_in, q_gain, k_gain):
    b, s = xn.shape[:2]
    z = xn @ w_in
    a, c = ATTN_WIDTH, CONV_WIDTH
    q, k, v, gate_b, gate_c, xc = jnp.split(z, [a, 2 * a, 3 * a, 3 * a + c, 3 * a + 2 * c], axis=-1)
    q = rms_norm(q.reshape(b, s, N_HEADS, 2, QK_HEAD_DIM), q_gain)
    k = rms_norm(k.reshape(b, s, N_HEADS, 2, QK_HEAD_DIM), k_gain)
    v = v.reshape(b, s, N_HEADS, V_HEAD_DIM)
    u = gate_c * xc
    return q, k, v, gate_b, u


def short_conv(u_ext, w_conv, n):
    y = u_ext[:, 0:n] * w_conv[0]
    for j in range(1, CONV_K):
        y = y + u_ext[:, j:j + n] * w_conv[j]
    return y


def prompt_attention(q, k, v, table, lam, lam_init, g_sub):
    b, s = q.shape[:2]
    nb = s // Q_BLOCK
    qf = q.astype(jnp.float32) * (QK_HEAD_DIM ** -0.5)
    qb = qf.reshape(b, nb, Q_BLOCK, N_HEADS, 2, QK_HEAD_DIM).transpose(1, 0, 3, 4, 2, 5)
    k_pos = jnp.arange(s, dtype=jnp.int32)

    def one_block(args):
        i, qi = args
        q_pos = i * Q_BLOCK + jnp.arange(Q_BLOCK, dtype=jnp.int32)
        mask = k_pos[None, :] <= q_pos[:, None]
        st = attn_partial(qi, k, v, rel_bias(table, q_pos, k_pos), mask)
        return diff_combine(st, lam, lam_init, g_sub)

    o = lax.map(one_block, (jnp.arange(nb, dtype=jnp.int32), qb))
    return o.transpose(1, 0, 2, 3, 4).reshape(b, s, ATTN_WIDTH)


def sample_attention(q, k_new, v_new, cache_k, cache_v, layer, page_table, table, lam, lam_init, g_sub):
    b, t = q.shape[:2]
    n_pages = page_table.shape[1]
    past = n_pages * PAGE_SIZE
    qf = (q.astype(jnp.float32) * (QK_HEAD_DIM ** -0.5)).transpose(0, 2, 3, 1, 4)
    q_pos = past + jnp.arange(t, dtype=jnp.int32)
    self_mask = q_pos[None, :] <= q_pos[:, None]
    stats = attn_partial(qf, k_new, v_new, rel_bias(table, q_pos, q_pos), self_mask)

    def step(carry, n):
        phys = page_table[:, n]
        kp = cache_k[layer, phys]
        vp = cache_v[layer, phys]
        k_pos = n * PAGE_SIZE + jnp.arange(PAGE_SIZE, dtype=jnp.int32)
        part = attn_partial(qf, kp, vp, rel_bias(table, q_pos, k_pos), None)
        return merge_stats(carry, part), None

    stats, _ = lax.scan(step, stats, jnp.arange(n_pages, dtype=jnp.int32))
    return diff_combine(stats, lam, lam_init, g_sub).reshape(b, t, ATTN_WIDTH)


def hier_moe(x, wg, bg, we, be, w_gate, w_up, w_down):
    shp = x.shape
    t = x.reshape(-1, D_MODEL)
    gl = (t @ wg).astype(jnp.float32) + bg.astype(jnp.float32)
    gp = jax.nn.softmax(gl, axis=-1)
    g_idx = jnp.argmax(gl, axis=-1)
    g_w = jnp.take_along_axis(gp, g_idx[:, None], axis=-1)
    el = (t @ we).astype(jnp.float32).reshape(-1, N_GROUPS, EXPERTS_PER_GROUP) + be.astype(jnp.float32)
    el_g = jnp.take_along_axis(el, g_idx[:, None, None], axis=1)[:, 0]
    top_v, top_i = lax.top_k(el_g, TOP_K_INNER)
    top_w = jax.nn.softmax(top_v, axis=-1) * g_w
    e_idx = g_idx[:, None] * EXPERTS_PER_GROUP + top_i
    gates = jnp.sum(jax.nn.one_hot(e_idx, N_EXPERTS, dtype=jnp.float32) * top_w[..., None], axis=1)
    h = jax.nn.silu(jnp.einsum('td,edf->tef', t, w_gate)) * jnp.einsum('td,edf->tef', t, w_up)
    h = h * gates[..., None].astype(h.dtype)
    y = jnp.einsum('tef,efd->td', h, w_down)
    return y.reshape(shp)


def setup_inputs(seed: int = 0) -> dict:
    key = jax.random.key(seed)
    ks = jax.random.split(key, 32)
    f32 = jnp.float32
    n_pages = PAST_LEN // PAGE_SIZE
    n_used = DEC_BATCH * n_pages
    n_pool = n_used + (n_used + 3) // 4
    nrm = lambda k, shape, s: jax.random.normal(k, shape, f32) * s
    page_table = jax.random.permutation(ks[0], n_pool)[:n_used].reshape(DEC_BATCH, n_pages).astype(jnp.int32)
    return {
        "x_prompt": nrm(ks[1], (BATCH, SEQ, D_MODEL), 1.0),
        "x_sample": nrm(ks[2], (DEC_BATCH, DEC_SEQ, D_MODEL), 1.0),
        "cache_k": nrm(ks[3], (DEPTH, n_pool, PAGE_SIZE, N_HEADS, 2, QK_HEAD_DIM), 1.0),
        "cache_v": nrm(ks[4], (DEPTH, n_pool, PAGE_SIZE, N_HEADS, V_HEAD_DIM), 1.0),
        "state_conv": nrm(ks[5], (DEPTH, DEC_BATCH, CONV_K - 1, CONV_WIDTH), 1.0),
        "page_table": page_table,
        "rel_table": nrm(ks[6], (N_BUCKETS, N_HEADS), 0.5),
        "attn_norm_g": 1.0 + nrm(ks[7], (DEPTH, D_MODEL), 0.02),
        "w_in": nrm(ks[8], (DEPTH, D_MODEL, PROJ_WIDTH), D_MODEL ** -0.5),
        "q_norm_g": 1.0 + nrm(ks[9], (DEPTH, QK_HEAD_DIM), 0.02),
        "k_norm_g": 1.0 + nrm(ks[10], (DEPTH, QK_HEAD_DIM), 0.02),
        "lambda_q": nrm(ks[11], (DEPTH, 2, QK_HEAD_DIM), 0.1),
        "lambda_k": nrm(ks[12], (DEPTH, 2, QK_HEAD_DIM), 0.1),
        "sub_norm_g": 1.0 + nrm(ks[13], (DEPTH, V_HEAD_DIM), 0.02),
        "conv_w": nrm(ks[14], (DEPTH, CONV_K, CONV_WIDTH), CONV_K ** -0.5),
        "w_o": nrm(ks[15], (DEPTH, D_MODEL, D_MODEL), D_MODEL ** -0.5),
        "ffn_norm_g": 1.0 + nrm(ks[16], (DEPTH, D_MODEL), 0.02),
        "router_group_w": nrm(ks[17], (DEPTH, D_MODEL, N_GROUPS), D_MODEL ** -0.5),
        "router_group_b": nrm(ks[18], (DEPTH, N_GROUPS), 0.01),
        "router_expert_w": nrm(ks[19], (DEPTH, D_MODEL, N_EXPERTS), D_MODEL ** -0.5),
        "router_expert_b": nrm(ks[20], (DEPTH, N_GROUPS, EXPERTS_PER_GROUP), 0.01),
        "w_gate": nrm(ks[21], (DEPTH, N_EXPERTS, D_MODEL, EXPERT_FF), D_MODEL ** -0.5),
        "w_up": nrm(ks[22], (DEPTH, N_EXPERTS, D_MODEL, EXPERT_FF), D_MODEL ** -0.5),
        "w_down": nrm(ks[23], (DEPTH, N_EXPERTS, EXPERT_FF, D_MODEL), EXPERT_FF ** -0.5),
    }


def reference(x_prompt, x_sample, cache_k, cache_v, state_conv, page_table, rel_table,
              attn_norm_g, w_in, q_norm_g, k_norm_g, lambda_q, lambda_k, sub_norm_g, conv_w, w_o,
              ffn_norm_g, router_group_w, router_group_b, router_expert_w, router_expert_b,
              w_gate, w_up, w_down):
    xp, xs = x_prompt, x_sample
    kp_l, vp_l, cp_l, ks_l, vs_l, cs_l = [], [], [], [], [], []
    for layer in range(DEPTH):
        lam_init = 0.8 - 0.6 * math.exp(-0.3 * layer)
        lq = lambda_q[layer].astype(jnp.float32)
        lk = lambda_k[layer].astype(jnp.float32)
        lam = jnp.exp(jnp.sum(lq[0] * lk[0])) - jnp.exp(jnp.sum(lq[1] * lk[1])) + lam_init
        moe_args = (router_group_w[layer], router_group_b[layer], router_expert_w[layer],
                    router_expert_b[layer], w_gate[layer], w_up[layer], w_down[layer])

        hp = rms_norm(xp, attn_norm_g[layer])
        q, k, v, gb, u = project(hp, w_in[layer], q_norm_g[layer], k_norm_g[layer])
        a = prompt_attention(q, k, v, rel_table, lam, lam_init, sub_norm_g[layer])
        u_ext = jnp.pad(u, ((0, 0), (CONV_K - 1, 0), (0, 0)))
        c = gb * short_conv(u_ext, conv_w[layer], u.shape[1])
        xp = xp + jnp.concatenate([a.astype(xp.dtype), c], axis=-1) @ w_o[layer]
        xp = xp + hier_moe(rms_norm(xp, ffn_norm_g[layer]), *moe_args)
        kp_l.append(k)
        vp_l.append(v)
        cp_l.append(u_ext[:, -(CONV_K - 1):])

        hs = rms_norm(xs, attn_norm_g[layer])
        q, k, v, gb, u = project(hs, w_in[layer], q_norm_g[layer], k_norm_g[layer])
        a = sample_attention(q, k, v, cache_k, cache_v, layer, page_table, rel_table,
                             lam, lam_init, sub_norm_g[layer])
        u_ext = jnp.concatenate([state_conv[layer].astype(u.dtype), u], axis=1)
        c = gb * short_conv(u_ext, conv_w[layer], u.shape[1])
        xs = xs + jnp.concatenate([a.astype(xs.dtype), c], axis=-1) @ w_o[layer]
        xs = xs + hier_moe(rms_norm(xs, ffn_norm_g[layer]), *moe_args)
        ks_l.append(k)
        vs_l.append(v)
        cs_l.append(u_ext[:, -(CONV_K - 1):])

    return (xp, xs, jnp.stack(kp_l), jnp.stack(vp_l), jnp.stack(cp_l),
            jnp.stack(ks_l), jnp.stack(vs_l), jnp.stack(cs_l))
```

```python
import functools
import math

import jax
import jax.numpy as jnp
from jax import lax
from jax.experimental import pallas as pl
from jax.experimental.pallas import tpu as pltpu

N_HEADS = 8
V_HEAD_DIM = 128
QK_HEAD_DIM = 64
CONV_K = 3
N_BUCKETS = 32
MAX_EXACT = N_BUCKETS // 2
MAX_DISTANCE = 2048
N_GROUPS = 4
EXPERTS_PER_GROUP = 4
N_EXPERTS = N_GROUPS * EXPERTS_PER_GROUP
RMS_EPS = 1e-6
NEG_INF = -1e30

LANES = 128
SUBLANES = 8
VMEM_PHYSICAL_BYTES = 64 * 1024 * 1024

F32 = jnp.float32
BF16 = jnp.bfloat16


def _vmem_limit(nbytes):
    return int(min(max(nbytes * 5 // 4 + (4 << 20), 16 << 20), VMEM_PHYSICAL_BYTES - (6 << 20)))


def _dot(a, b):
    return jnp.dot(a, b, preferred_element_type=F32)


def _dot_nt(a, b):
    return lax.dot_general(a, b, (((1,), (1,)), ((), ())), preferred_element_type=F32)


def _split_bf16(x):
    hi = x.astype(BF16)
    lo = (x - hi.astype(F32)).astype(BF16)
    return hi, lo


def _bias_kernel(table_ref, dist_ref, out_ref):
    d = dist_ref[...]
    n = jnp.maximum(d, 0)
    nf = jnp.maximum(n, 1).astype(F32)
    large = MAX_EXACT + (jnp.log(nf / MAX_EXACT) / math.log(MAX_DISTANCE / MAX_EXACT)
                         * (N_BUCKETS - MAX_EXACT)).astype(jnp.int32)
    large = jnp.minimum(large, N_BUCKETS - 1)
    bucket = jnp.where(n < MAX_EXACT, n, large)
    hits = [bucket == b for b in range(N_BUCKETS)]
    for h in range(N_HEADS):
        acc = jnp.zeros(d.shape, F32)
        for b in range(N_BUCKETS):
            acc = jnp.where(hits[b], table_ref[b, h], acc)
        out_ref[h] = jnp.where(d < 0, NEG_INF, acc)


def _bias_table(rel_table, dist):
    rows = dist.shape[0]
    tr = min(rows, 256)
    assert rows % tr == 0
    return pl.pallas_call(
        _bias_kernel,
        grid=(rows // tr,),
        in_specs=[pl.BlockSpec(memory_space=pltpu.SMEM),
                  pl.BlockSpec((tr, LANES), lambda i: (i, 0))],
        out_specs=pl.BlockSpec((N_HEADS, tr, LANES), lambda i: (0, i, 0)),
        out_shape=jax.ShapeDtypeStruct((N_HEADS, rows, LANES), F32),
        name="rel_bias",
    )(rel_table.astype(F32), dist)


def _group_rms_scale(z, gsum, gexp):
    hi, lo = _split_bf16(z * z)
    ssq = _dot(jnp.concatenate([hi, lo], axis=1), gsum)
    r = lax.rsqrt(ssq * (1.0 / QK_HEAD_DIM) + RMS_EPS)
    r_hi, r_lo = _split_bf16(r)
    return _dot(jnp.concatenate([r_hi, r_lo], axis=1), gexp)


def _proj_kernel(x_ref, g_ref, w_ref, qg_ref, kg_ref, gsum_ref, gexp_ref,
                 q_ref, kf_ref, kt_ref, vf_ref, vb_ref, gb_ref, u_ref, *, aw):
    x = x_ref[...]
    ms = jnp.mean(x * x, axis=-1, keepdims=True)
    xn = (x * lax.rsqrt(ms + RMS_EPS) * g_ref[...]).astype(BF16)
    gsum = gsum_ref[...]
    gexp = gexp_ref[...]

    zq = _dot(xn, w_ref[:, 0 * aw:1 * aw])
    q = zq * _group_rms_scale(zq, gsum, gexp) * qg_ref[...]
    q_ref[...] = (q * (QK_HEAD_DIM ** -0.5)).astype(BF16)

    zk = _dot(xn, w_ref[:, 1 * aw:2 * aw])
    k = zk * _group_rms_scale(zk, gsum, gexp) * kg_ref[...]
    kf_ref[...] = k
    kt_ref[...] = k.T.astype(BF16)

    v = _dot(xn, w_ref[:, 2 * aw:3 * aw])
    vf_ref[...] = v
    vb_ref[...] = v.astype(BF16)

    gb_ref[...] = _dot(xn, w_ref[:, 3 * aw:4 * aw])
    u_ref[...] = _dot(xn, w_ref[:, 4 * aw:5 * aw]) * _dot(xn, w_ref[:, 5 * aw:6 * aw])


def _project(x, g, w_bf16, q_gain, k_gain, gsum, gexp):
    rows, d = x.shape
    pw = w_bf16.shape[1]
    aw = pw // 6
    tm = min(rows, 256)
    assert rows % tm == 0 and aw % LANES == 0
    row_blk = lambda width: pl.BlockSpec((tm, width), lambda i: (i, 0))
    const = lambda shape: pl.BlockSpec(shape, lambda i: (0, 0))
    vmem = (d * pw * 2 + 2 * tm * d * 4 + 2 * tm * aw * (2 + 4 + 2 + 4 + 2 + 4 + 4)
            + 8 * tm * aw * 4 + 4 * aw * (2 * LANES + 2 * LANES))
    return pl.pallas_call(
        functools.partial(_proj_kernel, aw=aw),
        grid=(rows // tm,),
        in_specs=[row_blk(d), const((1, d)),
                  pl.BlockSpec((d, pw), lambda i: (0, 0), pipeline_mode=pl.Buffered(1)),
                  const((1, aw)), const((1, aw)),
                  const((2 * aw, LANES)), const((2 * LANES, aw))],
        out_specs=[row_blk(aw), row_blk(aw), pl.BlockSpec((aw, tm), lambda i: (0, i)),
                   row_blk(aw), row_blk(aw), row_blk(aw), row_blk(aw)],
        out_shape=[jax.ShapeDtypeStruct((rows, aw), BF16),
                   jax.ShapeDtypeStruct((rows, aw), F32),
                   jax.ShapeDtypeStruct((aw, rows), BF16),
                   jax.ShapeDtypeStruct((rows, aw), F32),
                   jax.ShapeDtypeStruct((rows, aw), BF16),
                   jax.ShapeDtypeStruct((rows, aw), F32),
                   jax.ShapeDtypeStruct((rows, aw), F32)],
        compiler_params=pltpu.CompilerParams(
            dimension_semantics=("arbitrary",), vmem_limit_bytes=_vmem_limit(vmem)),
        name="in_proj",
    )(x, g, w_bf16, q_gain, k_gain, gsum, gexp)


def _lambda(lq_ref, lk_ref, lam_init):
    e = jnp.exp(jnp.sum(lq_ref[...] * lk_ref[...], axis=-1, keepdims=True))
    return e[0:1] - e[1:2] + lam_init


def _prompt_attn_kernel(q_ref, kt_ref, v_ref, bias_ref, lq_ref, lk_ref, gs_ref,
                        o_ref, m_scr, l_scr, acc_scr, *, t, nd, lam_init):
    qi = pl.program_id(1)
    q = q_ref[...]
    lane = lax.broadcasted_iota(jnp.int32, q.shape, 1)
    zero = jnp.zeros_like(q)
    qs = jnp.concatenate([jnp.where(lane < QK_HEAD_DIM, q, zero),
                          jnp.where(lane >= QK_HEAD_DIM, q, zero)], axis=0)
    m_scr[...] = jnp.full(m_scr.shape, NEG_INF, F32)
    l_scr[...] = jnp.zeros(l_scr.shape, F32)
    acc_scr[...] = jnp.zeros(acc_scr.shape, F32)

    def body(ki, carry):
        start = pl.multiple_of(ki * t, t)
        s = _dot(qs, kt_ref[:, pl.ds(start, t)])
        b = bias_ref[0, jnp.minimum(qi - ki, nd)]
        s = (s.reshape(2, t, t) + b[None]).reshape(2 * t, t)
        m_prev = m_scr[...]
        m_new = jnp.maximum(m_prev, jnp.max(s, axis=-1, keepdims=True))
        p = jnp.exp(s - m_new)
        alpha = jnp.exp(m_prev - m_new)
        l_scr[...] = alpha * l_scr[...] + jnp.sum(p, axis=-1, keepdims=True)
        acc_scr[...] = alpha * acc_scr[...] + _dot(p.astype(BF16), v_ref[pl.ds(start, t), :])
        m_scr[...] = m_new
        return carry

    lax.fori_loop(0, qi + 1, body, 0)

    o = acc_scr[...] / l_scr[...]
    lam = _lambda(lq_ref, lk_ref, lam_init)
    o = o[:t] - lam * o[t:]
    o = o * lax.rsqrt(jnp.mean(o * o, axis=-1, keepdims=True) + RMS_EPS)
    o_ref[...] = (o * gs_ref[...] * (1.0 - lam_init)).astype(o_ref.dtype)


def _prompt_attention(q, kt, vb, bias, lq, lk, g_sub, *, t, nd, lam_init):
    s, aw = q.shape
    vmem = (2 * t * LANES * 2 + 2 * LANES * s * 2 + 2 * s * LANES * 2
            + 2 * (nd + 1) * t * t * 4 + 2 * t * LANES * 2
            + 2 * t * (2 * LANES + LANES) * 4 + 6 * 2 * t * t * 4)
    return pl.pallas_call(
        functools.partial(_prompt_attn_kernel, t=t, nd=nd, lam_init=lam_init),
        grid=(N_HEADS, s // t),
        in_specs=[pl.BlockSpec((t, V_HEAD_DIM), lambda h, i: (i, h)),
                  pl.BlockSpec((V_HEAD_DIM, s), lambda h, i: (h, 0)),
                  pl.BlockSpec((s, V_HEAD_DIM), lambda h, i: (0, h)),
                  pl.BlockSpec((1, nd + 1, t, t), lambda h, i: (h, 0, 0, 0)),
                  pl.BlockSpec((2, QK_HEAD_DIM), lambda h, i: (0, 0)),
                  pl.BlockSpec((2, QK_HEAD_DIM), lambda h, i: (0, 0)),
                  pl.BlockSpec((1, V_HEAD_DIM), lambda h, i: (0, 0))],
        out_specs=pl.BlockSpec((t, V_HEAD_DIM), lambda h, i: (i, h)),
        out_shape=jax.ShapeDtypeStruct((s, aw), BF16),
        scratch_shapes=[pltpu.VMEM((2 * t, 1), F32), pltpu.VMEM((2 * t, 1), F32),
                        pltpu.VMEM((2 * t, V_HEAD_DIM), F32)],
        compiler_params=pltpu.CompilerParams(
            dimension_semantics=("arbitrary", "arbitrary"),
            vmem_limit_bytes=_vmem_limit(vmem)),
        name="prompt_attn",
    )(q, kt, vb, bias, lq, lk, g_sub)


def _sample_attn_kernel(pt_ref, qr_ref, kn_ref, vn_ref, bias_ref, bself_ref, lq_ref, lk_ref,
                        gs_ref, *rest, pages, page, lam_init):
    del pt_ref
    k_refs = rest[:pages]
    v_refs = rest[pages:2 * pages]
    o_ref, m_scr, l_scr, acc_scr = rest[2 * pages:]
    step = pl.program_id(1)
    qr = qr_ref[0]
    half = qr.shape[0] // 2

    def both_maps(b):
        return jnp.concatenate([b, b], axis=0)

    def update(s, vs):
        m_prev = m_scr[...]
        m_new = jnp.maximum(m_prev, jnp.max(s, axis=-1, keepdims=True))
        p = jnp.exp(s - m_new)
        alpha = jnp.exp(m_prev - m_new)
        l_scr[...] = alpha * l_scr[...] + jnp.sum(p, axis=-1, keepdims=True)
        p = p.astype(BF16)
        acc = alpha * acc_scr[...]
        for j, vj in enumerate(vs):
            acc = acc + _dot(p[:, j * page:(j + 1) * page], vj)
        acc_scr[...] = acc
        m_scr[...] = m_new

    @pl.when(step == 0)
    def _():
        m_scr[...] = jnp.full(m_scr.shape, NEG_INF, F32)
        l_scr[...] = jnp.zeros(l_scr.shape, F32)
        acc_scr[...] = jnp.zeros(acc_scr.shape, F32)
        update(_dot_nt(qr, kn_ref[0]) + both_maps(bself_ref[...]), [vn_ref[0]])

    s = jnp.concatenate(
        [_dot_nt(qr, k_refs[j][0].astype(BF16)) + both_maps(bias_ref[j]) for j in range(pages)],
        axis=1)
    update(s, [v_refs[j][0].astype(BF16) for j in range(pages)])

    @pl.when(step == pl.num_programs(1) - 1)
    def _():
        o = acc_scr[...] / l_scr[...]
        lam = _lambda(lq_ref, lk_ref, lam_init)
        o = o[:half] - lam * o[half:]
        row_head = lax.broadcasted_iota(jnp.int32, o.shape, 0) % N_HEADS
        lane_head = lax.broadcasted_iota(jnp.int32, o.shape, 1) // V_HEAD_DIM
        o = jnp.where(row_head == lane_head, o, 0.0)
        ms = jnp.sum(o * o, axis=-1, keepdims=True) * (1.0 / V_HEAD_DIM)
        o = o * lax.rsqrt(ms + RMS_EPS)
        o = jnp.sum(o.reshape(half // N_HEADS, N_HEADS, o.shape[1]), axis=1)
        o_ref[0] = (o * gs_ref[...] * (1.0 - lam_init)).astype(o_ref.dtype)


def _sample_attention(page_table, q_rows, k_new, v_new, bias_pages, bias_self, lq, lk, g_sub_t,
                      cache_k, cache_v, *, pages, lam_init):
    nb, n_pages = page_table.shape
    _, page, aw = cache_k.shape
    dt = q_rows.shape[1] // (2 * N_HEADS)
    steps = n_pages // pages
    assert n_pages % pages == 0
    rows = q_rows.shape[1]

    def kv_spec(j):
        return pl.BlockSpec((1, page, aw),
                            lambda b, s, pt: (pt[b * n_pages + s * pages + j], 0, 0))

    per_b = lambda shape: pl.BlockSpec(shape, lambda b, s, pt: (b, 0, 0))
    const2 = lambda shape: pl.BlockSpec(shape, lambda b, s, pt: (0, 0))
    vmem = (2 * 2 * pages * page * aw * 4 + 2 * pages * page * aw * 2
            + 4 * rows * aw * 4 + 2 * 3 * rows * aw * 2)
    grid_spec = pltpu.PrefetchScalarGridSpec(
        num_scalar_prefetch=1,
        grid=(nb, steps),
        in_specs=[per_b((1, rows, aw)), per_b((1, page, aw)), per_b((1, page, aw)),
                  pl.BlockSpec((pages, rows // 2, page), lambda b, s, pt: (s, 0, 0)),
                  const2((rows // 2, page)),
                  const2((2, QK_HEAD_DIM)), const2((2, QK_HEAD_DIM)), const2((1, aw))]
                 + [kv_spec(j) for j in range(pages)] + [kv_spec(j) for j in range(pages)],
        out_specs=pl.BlockSpec((1, dt, aw), lambda b, s, pt: (b, 0, 0)),
        scratch_shapes=[pltpu.VMEM((rows, 1), F32), pltpu.VMEM((rows, 1), F32),
                        pltpu.VMEM((rows, aw), F32)])
    return pl.pallas_call(
        functools.partial(_sample_attn_kernel, pages=pages, page=page, lam_init=lam_init),
        grid_spec=grid_spec,
        out_shape=jax.ShapeDtypeStruct((nb, dt, aw), BF16),
        compiler_params=pltpu.CompilerParams(
            dimension_semantics=("arbitrary", "arbitrary"),
            vmem_limit_bytes=_vmem_limit(vmem)),
        name="sample_attn",
    )(page_table.reshape(-1), q_rows, k_new, v_new, bias_pages, bias_self, lq, lk, g_sub_t,
      *([cache_k] * pages), *([cache_v] * pages))


def _out_proj_kernel(*refs, halo):
    if halo:
        a_ref, gb_ref, u_ref, prev_ref, cw_ref, wo_ref, x_ref, o_ref = refs
        u = u_ref[...]
        first = pl.program_id(0) == 0
        prev = jnp.where(first, 0.0, prev_ref[...])
        ucat = jnp.concatenate([prev, u], axis=0)
        p1 = pltpu.roll(ucat, 1, axis=0)[SUBLANES:]
        p2 = pltpu.roll(ucat, 2, axis=0)[SUBLANES:]
    else:
        a_ref, gb_ref, u_ref, p1_ref, p2_ref, cw_ref, wo_ref, x_ref, o_ref = refs
        u, p1, p2 = u_ref[...], p1_ref[...], p2_ref[...]
    cw = cw_ref[...]
    aw = u.shape[1]
    c = gb_ref[...] * (p2 * cw[0:1] + p1 * cw[1:2] + u * cw[2:3])
    y = _dot(a_ref[...], wo_ref[:aw, :]) + _dot(c.astype(BF16), wo_ref[aw:, :])
    o_ref[...] = x_ref[...] + y


def _out_project(a, gb, u, conv_w, wo_bf16, x, prev=None):
    rows, d = x.shape
    aw = a.shape[1]
    tm = min(rows, 256)
    assert rows % tm == 0
    row_blk = lambda width: pl.BlockSpec((tm, width), lambda i: (i, 0))
    const = lambda shape: pl.BlockSpec(shape, lambda i: (0, 0))
    halo = prev is None
    if halo:
        nsub = tm // SUBLANES
        extra = [pl.BlockSpec((SUBLANES, aw), lambda i: (jnp.maximum(i * nsub - 1, 0), 0))]
        extra_args = [u]
    else:
        extra = [row_blk(aw), row_blk(aw)]
        extra_args = list(prev)
    vmem = 2 * d * d * 2 + 2 * tm * (aw * (2 + 4 + 4 + 8) + d * 8) + 6 * tm * d * 4
    return pl.pallas_call(
        functools.partial(_out_proj_kernel, halo=halo),
        grid=(rows // tm,),
        in_specs=[row_blk(aw), row_blk(aw), row_blk(aw)] + extra
                 + [const((CONV_K, aw)), const((d, d)), row_blk(d)],
        out_specs=row_blk(d),
        out_shape=jax.ShapeDtypeStruct((rows, d), F32),
        compiler_params=pltpu.CompilerParams(
            dimension_semantics=("arbitrary",), vmem_limit_bytes=_vmem_limit(vmem)),
        name="out_proj",
    )(a, gb, u, *extra_args, conv_w, wo_bf16, x)


ROUTER_ROWS = 32
EXPERT_ROW0 = 8


def _router_kernel(x_ref, g_ref, w3_ref, bias_ref, idx_ref, wt_ref):
    x = x_ref[...]
    ms = jnp.mean(x * x, axis=-1, keepdims=True)
    xn = x * lax.rsqrt(ms + RMS_EPS) * g_ref[...]
    hi, lo = _split_bf16(xn)
    logits = _dot_nt(w3_ref[...], jnp.concatenate([hi, hi, lo], axis=1)) + bias_ref[...]
    big = jnp.int32(1 << 20)

    gl = logits[0:SUBLANES]
    grow = lax.broadcasted_iota(jnp.int32, gl.shape, 0)
    gmax = jnp.max(gl, axis=0, keepdims=True)
    g_idx = jnp.min(jnp.where(gl == gmax, grow, big), axis=0, keepdims=True)
    g_w = 1.0 / jnp.sum(jnp.exp(gl - gmax), axis=0, keepdims=True)

    el = logits[EXPERT_ROW0:EXPERT_ROW0 + N_EXPERTS]
    erow = lax.broadcasted_iota(jnp.int32, el.shape, 0)
    el = jnp.where(erow // EXPERTS_PER_GROUP == g_idx, el, NEG_INF)
    v1 = jnp.max(el, axis=0, keepdims=True)
    i1 = jnp.min(jnp.where(el == v1, erow, big), axis=0, keepdims=True)
    el2 = jnp.where(erow == i1, NEG_INF, el)
    v2 = jnp.max(el2, axis=0, keepdims=True)
    i2 = jnp.min(jnp.where(el2 == v2, erow, big), axis=0, keepdims=True)
    t = jnp.exp(v2 - v1)
    w1 = g_w / (1.0 + t)
    idx_ref[...] = jnp.concatenate([i1, i2], axis=0)
    wt_ref[...] = jnp.concatenate([w1, w1 * t], axis=0)


def _route(x, g, w3, bias_col):
    rows, d = x.shape
    tm = min(rows, 256)
    assert rows % tm == 0
    return pl.pallas_call(
        _router_kernel,
        grid=(rows // tm,),
        in_specs=[pl.BlockSpec((tm, d), lambda i: (i, 0)),
                  pl.BlockSpec((1, d), lambda i: (0, 0)),
                  pl.BlockSpec((ROUTER_ROWS, 3 * d), lambda i: (0, 0)),
                  pl.BlockSpec((ROUTER_ROWS, 1), lambda i: (0, 0))],
        out_specs=[pl.BlockSpec((2, tm), lambda i: (0, i)),
                   pl.BlockSpec((2, tm), lambda i: (0, i))],
        out_shape=[jax.ShapeDtypeStruct((2, rows), jnp.int32),
                   jax.ShapeDtypeStruct((2, rows), F32)],
        compiler_params=pltpu.CompilerParams(dimension_semantics=("arbitrary",)),
        name="router",
    )(x, g, w3, bias_col)


def _expert_kernel(te_ref, tv_ref, src_ref, dst_ref, roww_ref, g_ref, wg_ref, wu_ref, wd_ref,
                   xp_hbm, xs_hbm, y_hbm, xbuf, ybuf, gsem, ssem, *, tm, rp):
    del te_ref
    i = pl.program_id(0)

    def gather(r, src):
        from_prompt = src < rp

        @pl.when(from_prompt)
        def _():
            pltpu.make_async_copy(xp_hbm.at[pl.ds(src, 1)], xbuf.at[pl.ds(r, 1)], gsem).start()

        @pl.when(jnp.logical_not(from_prompt))
        def _():
            pltpu.make_async_copy(xs_hbm.at[pl.ds(src - rp, 1)], xbuf.at[pl.ds(r, 1)],
                                  gsem).start()

    @pl.when(tv_ref[i] > 0)
    def _():
        def issue(r, c):
            gather(r, src_ref[0, 0, r])
            return c
        lax.fori_loop(0, tm, issue, 0)

        def drain(r, c):
            pltpu.make_async_copy(xp_hbm.at[pl.ds(0, 1)], xbuf.at[pl.ds(r, 1)], gsem).wait()
            return c
        lax.fori_loop(0, tm, drain, 0)

        x = xbuf[...]
        ms = jnp.mean(x * x, axis=-1, keepdims=True)
        xn = (x * lax.rsqrt(ms + RMS_EPS) * g_ref[...]).astype(BF16)
        hg = _dot(xn, wg_ref[0].astype(BF16))
        hu = _dot(xn, wu_ref[0].astype(BF16))
        h = hg * jax.nn.sigmoid(hg) * hu * roww_ref[...]
        ybuf[...] = _dot(h.astype(BF16), wd_ref[0].astype(BF16))

        def scatter(r, c):
            dst = dst_ref[0, 0, r]

            @pl.when(dst >= 0)
            def _():
                pltpu.make_async_copy(ybuf.at[pl.ds(r, 1)], y_hbm.at[pl.ds(dst, 1)], ssem).start()
            return c
        lax.fori_loop(0, tm, scatter, 0)

        def settle(r, c):
            @pl.when(dst_ref[0, 0, r] >= 0)
            def _():
                pltpu.make_async_copy(ybuf.at[pl.ds(r, 1)], y_hbm.at[pl.ds(0, 1)], ssem).wait()
            return c
        lax.fori_loop(0, tm, settle, 0)


def _experts(tile_expert, tile_valid, row_src, row_dst, row_w, g, w_gate, w_up, w_down,
             x_prompt, x_sample, *, tm):
    nt = tile_expert.shape[0]
    rp, d = x_prompt.shape
    rs = x_sample.shape[0]
    ff = w_gate.shape[2]
    grid_spec = pltpu.PrefetchScalarGridSpec(
        num_scalar_prefetch=2,
        grid=(nt,),
        in_specs=[pl.BlockSpec((1, 1, tm), lambda i, te, tv: (i, 0, 0), memory_space=pltpu.SMEM),
                  pl.BlockSpec((1, 1, tm), lambda i, te, tv: (i, 0, 0), memory_space=pltpu.SMEM),
                  pl.BlockSpec((tm, 1), lambda i, te, tv: (i, 0)),
                  pl.BlockSpec((1, d), lambda i, te, tv: (0, 0)),
                  pl.BlockSpec((1, d, ff), lambda i, te, tv: (te[i], 0, 0)),
                  pl.BlockSpec((1, d, ff), lambda i, te, tv: (te[i], 0, 0)),
                  pl.BlockSpec((1, ff, d), lambda i, te, tv: (te[i], 0, 0)),
                  pl.BlockSpec(memory_space=pl.ANY),
                  pl.BlockSpec(memory_space=pl.ANY)],
        out_specs=pl.BlockSpec(memory_space=pl.ANY),
        scratch_shapes=[pltpu.VMEM((tm, d), F32), pltpu.VMEM((tm, d), F32),
                        pltpu.SemaphoreType.DMA(()), pltpu.SemaphoreType.DMA(())])
    vmem = 2 * 3 * d * ff * 4 + 3 * d * ff * 2 + 2 * tm * d * 4 + 6 * tm * d * 4
    return pl.pallas_call(
        functools.partial(_expert_kernel, tm=tm, rp=rp),
        grid_spec=grid_spec,
        out_shape=jax.ShapeDtypeStruct((2 * (rp + rs), d), F32),
        compiler_params=pltpu.CompilerParams(
            dimension_semantics=("arbitrary",), vmem_limit_bytes=_vmem_limit(vmem)),
        name="experts",
    )(tile_expert, tile_valid, row_src, row_dst, row_w, g, w_gate, w_up, w_down,
      x_prompt, x_sample)


def _combine_kernel(x_ref, y0_ref, y1_ref, o_ref):
    o_ref[...] = x_ref[...] + (y0_ref[...] + y1_ref[...])


def _combine(x, y2, row0, total_rows):
    rows, d = x.shape
    tm = min(rows, 128)
    assert rows % tm == 0 and row0 % tm == 0 and total_rows % tm == 0
    b0 = row0 // tm
    b1 = (total_rows + row0) // tm
    return pl.pallas_call(
        _combine_kernel,
        grid=(rows // tm,),
        in_specs=[pl.BlockSpec((tm, d), lambda i: (i, 0)),
                  pl.BlockSpec((tm, d), lambda i: (b0 + i, 0)),
                  pl.BlockSpec((tm, d), lambda i: (b1 + i, 0))],
        out_specs=pl.BlockSpec((tm, d), lambda i: (i, 0)),
        out_shape=jax.ShapeDtypeStruct((rows, d), F32),
        compiler_params=pltpu.CompilerParams(dimension_semantics=("arbitrary",)),
        name="moe_combine",
    )(x, y2, y2)


def _routing_plan(e_idx, w, tm):
    total = e_idx.shape[1]
    n_assign = 2 * total
    nt = -(-n_assign // tm) + N_EXPERTS
    e_flat = e_idx.reshape(-1)
    w_flat = w.reshape(-1)
    order = jnp.argsort(e_flat, stable=True).astype(jnp.int32)
    sorted_e = e_flat[order]
    counts = jnp.zeros((N_EXPERTS,), jnp.int32).at[e_flat].add(1)
    tiles_per = (counts + tm - 1) // tm
    tile_end = jnp.cumsum(tiles_per)
    tile_start = tile_end - tiles_per
    group_start = jnp.cumsum(counts) - counts
    prow = tile_start[sorted_e] * tm + (jnp.arange(n_assign, dtype=jnp.int32) - group_start[sorted_e])
    row_src = jnp.zeros((nt * tm,), jnp.int32).at[prow].set(order % total)
    row_dst = jnp.full((nt * tm,), -1, jnp.int32).at[prow].set(order)
    row_w = jnp.zeros((nt * tm,), F32).at[prow].set(w_flat[order])
    tiles = jnp.arange(nt, dtype=jnp.int32)
    tile_expert = jnp.minimum(jnp.searchsorted(tile_end, tiles, side="right"),
                              N_EXPERTS - 1).astype(jnp.int32)
    tile_valid = (tiles < tile_end[-1]).astype(jnp.int32)
    return (tile_expert, tile_valid, row_src.reshape(nt, 1, tm), row_dst.reshape(nt, 1, tm),
            row_w.reshape(nt * tm, 1))


def _group_matrices(aw):
    e = jnp.arange(aw, dtype=jnp.int32) // QK_HEAD_DIM
    lanes = jnp.arange(LANES, dtype=jnp.int32)
    member = (e[:, None] == lanes[None, :]).astype(BF16)
    return jnp.concatenate([member, member], axis=0), jnp.concatenate([member.T, member.T], axis=0)


def _prompt_bias(rel_table, t, nd):
    i = jnp.arange(t, dtype=jnp.int32)
    dist = (jnp.arange(nd + 1, dtype=jnp.int32)[:, None, None] * t + i[None, :, None]
            - i[None, None, :])
    bias = _bias_table(rel_table, dist.reshape(-1, LANES))
    return bias.reshape(N_HEADS, nd + 1, t, t)


def _sample_bias(rel_table, past, n_pages, page, dt):
    tq = jnp.arange(dt, dtype=jnp.int32)
    kpos = jnp.arange(n_pages * page, dtype=jnp.int32).reshape(n_pages, 1, page)
    dist_pages = (past + tq[None, :, None] - kpos).reshape(-1, page)
    j = jnp.arange(page, dtype=jnp.int32)
    self_rows = jnp.where(j[None, :] < dt, tq[:, None] - j[None, :], -1)
    pad = jnp.full((SUBLANES - dt % SUBLANES if dt % SUBLANES else 0, page), -1, jnp.int32)
    dist = jnp.concatenate([dist_pages, self_rows, pad], axis=0)
    rows = dist.shape[0]
    rpad = (-rows) % 256 if rows > 256 else 0
    dist = jnp.concatenate([dist, jnp.full((rpad, page), -1, jnp.int32)], axis=0)
    bias = _bias_table(rel_table, dist)
    bp = bias[:, :n_pages * dt].reshape(N_HEADS, n_pages, dt, page)
    bp = bp.transpose(1, 2, 0, 3).reshape(n_pages, dt * N_HEADS, page)
    bs = bias[:, n_pages * dt:n_pages * dt + dt].transpose(1, 0, 2).reshape(dt * N_HEADS, page)
    return bp, bs


def kernel(x_prompt, x_sample, cache_k, cache_v, state_conv, page_table, rel_table, attn_norm_g,
           w_in, q_norm_g, k_norm_g, lambda_q, lambda_k, sub_norm_g, conv_w, w_o, ffn_norm_g,
           router_group_w, router_group_b, router_expert_w, router_expert_b, w_gate, w_up, w_down):
    depth = w_in.shape[0]
    assert depth == 1, "single-layer trunk"
    batch, seq, d = x_prompt.shape
    assert batch == 1, "one prompt sequence"
    nb, dt, _ = x_sample.shape
    n_pool, page = cache_k.shape[1], cache_k.shape[2]
    n_pages = page_table.shape[1]
    past = n_pages * page
    aw = N_HEADS * V_HEAD_DIM
    layer = 0
    lam_init = 0.8 - 0.6 * math.exp(-0.3 * layer)

    w_in_b = w_in[layer].astype(BF16)
    w_o_b = w_o[layer].astype(BF16)
    g_attn = attn_norm_g[layer].reshape(1, d)
    g_ffn = ffn_norm_g[layer].reshape(1, d)
    qg = jnp.tile(q_norm_g[layer], aw // QK_HEAD_DIM).reshape(1, aw)
    kg = jnp.tile(k_norm_g[layer], aw // QK_HEAD_DIM).reshape(1, aw)
    g_sub = sub_norm_g[layer].reshape(1, V_HEAD_DIM)
    g_sub_t = jnp.tile(sub_norm_g[layer], N_HEADS).reshape(1, aw)
    gsum, gexp = _group_matrices(aw)
    lq, lk = lambda_q[layer], lambda_k[layer]
    cw = conv_w[layer]

    xp = x_prompt.reshape(seq, d)
    xs = x_sample.reshape(nb * dt, d)

    qp, kp, ktp, vp, vbp, gbp, up = _project(xp, g_attn, w_in_b, qg, kg, gsum, gexp)
    qs_, ks, _, vs, _, gbs, us = _project(xs, g_attn, w_in_b, qg, kg, gsum, gexp)

    t = min(seq, 512)
    assert seq % t == 0
    nd = min(-(-(MAX_DISTANCE + t - 1) // t), seq // t)
    bias_p = _prompt_bias(rel_table, t, nd)
    ap = _prompt_attention(qp, ktp, vbp, bias_p, lq, lk, g_sub, t=t, nd=nd, lam_init=lam_init)

    bias_pages, bias_self = _sample_bias(rel_table, past, n_pages, page, dt)
    lane_hc = jnp.arange(aw, dtype=jnp.int32) // QK_HEAD_DIM
    row_c = jnp.arange(2, dtype=jnp.int32)[:, None, None]
    row_h = jnp.arange(N_HEADS, dtype=jnp.int32)[None, None, :]
    sel = (lane_hc[None, None, None, :] == (row_h * 2 + row_c)[..., None])
    q3 = qs_.reshape(nb, 1, dt, 1, aw)
    q_rows = jnp.where(sel[None], q3, jnp.zeros((), BF16)).reshape(nb, 2 * dt * N_HEADS, aw)
    pad_keys = lambda z: jnp.pad(z.reshape(nb, dt, aw).astype(BF16),
                                 ((0, 0), (0, page - dt), (0, 0)))
    pages_per_step = math.gcd(n_pages, 8)
    a_s = _sample_attention(page_table, q_rows, pad_keys(ks), pad_keys(vs), bias_pages, bias_self,
                            lq, lk, g_sub_t, cache_k.reshape(n_pool, page, aw),
                            cache_v.reshape(n_pool, page, aw),
                            pages=pages_per_step, lam_init=lam_init)
    a_s = a_s.reshape(nb * dt, aw)

    x2p = _out_project(ap, gbp, up, cw, w_o_b, xp)
    u_ext = jnp.concatenate([state_conv[layer].astype(F32), us.reshape(nb, dt, aw)], axis=1)
    prev1 = u_ext[:, 1:1 + dt].reshape(nb * dt, aw)
    prev2 = u_ext[:, 0:dt].reshape(nb * dt, aw)
    x2s = _out_project(a_s, gbs, us, cw, w_o_b, xs, prev=(prev1, prev2))

    wr = jnp.zeros((ROUTER_ROWS, d), F32)
    wr = wr.at[0:N_GROUPS].set(router_group_w[layer].T)
    wr = wr.at[EXPERT_ROW0:EXPERT_ROW0 + N_EXPERTS].set(router_expert_w[layer].T)
    wr_hi, wr_lo = _split_bf16(wr)
    w3 = jnp.concatenate([wr_hi, wr_lo, wr_hi], axis=1)
    bias_col = jnp.full((ROUTER_ROWS,), NEG_INF, F32)
    bias_col = bias_col.at[0:N_GROUPS].set(router_group_b[layer].astype(F32))
    bias_col = bias_col.at[EXPERT_ROW0:EXPERT_ROW0 + N_EXPERTS].set(
        router_expert_b[layer].reshape(-1).astype(F32)).reshape(ROUTER_ROWS, 1)
    idx_p, wt_p = _route(x2p, g_ffn, w3, bias_col)
    idx_s, wt_s = _route(x2s, g_ffn, w3, bias_col)
    e_idx = jnp.concatenate([idx_p, idx_s], axis=1)
    wts = jnp.concatenate([wt_p, wt_s], axis=1)
    tm_e = 256
    plan = _routing_plan(e_idx, wts, tm_e)
    y2 = _experts(*plan, g_ffn, w_gate[layer], w_up[layer], w_down[layer], x2p, x2s, tm=tm_e)
    total = seq + nb * dt
    yp = _combine(x2p, y2, 0, total)
    ys = _combine(x2s, y2, seq, total)

    return (yp.reshape(batch, seq, d),
            ys.reshape(nb, dt, d),
            kp.reshape(depth, batch, seq, N_HEADS, 2, QK_HEAD_DIM),
            vp.reshape(depth, batch, seq, N_HEADS, V_HEAD_DIM),
            up[seq - (CONV_K - 1):].reshape(depth, batch, CONV_K - 1, aw),
            ks.reshape(depth, nb, dt, N_HEADS, 2, QK_HEAD_DIM),
            vs.reshape(depth, nb, dt, N_HEADS, V_HEAD_DIM),
            u_ext[:, dt:].reshape(depth, nb, CONV_K - 1, aw))
```

```python
import functools
import math

import jax
import jax.numpy as jnp
from jax import lax
from jax.experimental import pallas as pl
from jax.experimental.pallas import tpu as pltpu

N_HEADS = 8
V_HEAD_DIM = 128
QK_HEAD_DIM = 64
CONV_K = 3
N_BUCKETS = 32
MAX_EXACT = N_BUCKETS // 2
MAX_DISTANCE = 2048
N_GROUPS = 4
EXPERTS_PER_GROUP = 4
N_EXPERTS = N_GROUPS * EXPERTS_PER_GROUP
RMS_EPS = 1e-6
NEG_INF = -1e30
LOG2E = math.log2(math.e)

LANES = 128
SUBLANES = 8
VMEM_PHYSICAL_BYTES = 64 * 1024 * 1024

F32 = jnp.float32
BF16 = jnp.bfloat16


def _vmem_limit(nbytes):
    return int(min(max(nbytes * 5 // 4 + (4 << 20), 16 << 20), VMEM_PHYSICAL_BYTES - (6 << 20)))


def _dot(a, b):
    return jnp.dot(a, b, preferred_element_type=F32)


def _dot_nt(a, b):
    return lax.dot_general(a, b, (((1,), (1,)), ((), ())), preferred_element_type=F32)


def _split_bf16(x):
    hi = x.astype(BF16)
    lo = (x - hi.astype(F32)).astype(BF16)
    return hi, lo


def _bias_kernel(table_ref, dist_ref, out_ref):
    d = dist_ref[...]
    n = jnp.maximum(d, 0)
    nf = jnp.maximum(n, 1).astype(F32)
    large = MAX_EXACT + (jnp.log(nf / MAX_EXACT) / math.log(MAX_DISTANCE / MAX_EXACT)
                         * (N_BUCKETS - MAX_EXACT)).astype(jnp.int32)
    large = jnp.minimum(large, N_BUCKETS - 1)
    bucket = jnp.where(n < MAX_EXACT, n, large)
    hits = [bucket == b for b in range(N_BUCKETS)]
    for h in range(N_HEADS):
        acc = jnp.zeros(d.shape, F32)
        for b in range(N_BUCKETS):
            acc = jnp.where(hits[b], table_ref[b, h] * LOG2E, acc)
        out_ref[h] = jnp.where(d < 0, NEG_INF, acc)


def _bias_table(rel_table, dist):
    rows = dist.shape[0]
    tr = min(rows, 256)
    assert rows % tr == 0
    return pl.pallas_call(
        _bias_kernel,
        grid=(rows // tr,),
        in_specs=[pl.BlockSpec(memory_space=pltpu.SMEM),
                  pl.BlockSpec((tr, LANES), lambda i: (i, 0))],
        out_specs=pl.BlockSpec((N_HEADS, tr, LANES), lambda i: (0, i, 0)),
        out_shape=jax.ShapeDtypeStruct((N_HEADS, rows, LANES), F32),
        name="rel_bias",
    )(rel_table.astype(F32), dist)


def _group_rms_scale(z, gsum, gexp):
    hi, lo = _split_bf16(z * z)
    ssq = _dot(jnp.concatenate([hi, lo], axis=1), gsum)
    r = lax.rsqrt(ssq * (1.0 / QK_HEAD_DIM) + RMS_EPS)
    r_hi, r_lo = _split_bf16(r)
    return _dot(jnp.concatenate([r_hi, r_lo], axis=1), gexp)


def _proj_kernel(x_ref, g_ref, w_ref, qg_ref, kg_ref, gsum_ref, gexp_ref,
                 qt_ref, kf_ref, kb_ref, vf_ref, vt_ref, gb_ref, u_ref, *, aw):
    x = x_ref[...]
    ms = jnp.mean(x * x, axis=-1, keepdims=True)
    xn = (x * lax.rsqrt(ms + RMS_EPS) * g_ref[...]).astype(BF16)
    gsum = gsum_ref[...]
    gexp = gexp_ref[...]

    zq = _dot(xn, w_ref[:, 0 * aw:1 * aw])
    q = zq * _group_rms_scale(zq, gsum, gexp) * qg_ref[...]
    qt_ref[...] = (q * (QK_HEAD_DIM ** -0.5 * LOG2E)).T.astype(BF16)

    zk = _dot(xn, w_ref[:, 1 * aw:2 * aw])
    k = zk * _group_rms_scale(zk, gsum, gexp) * kg_ref[...]
    kf_ref[...] = k
    kb_ref[...] = k.astype(BF16)

    v = _dot(xn, w_ref[:, 2 * aw:3 * aw])
    vf_ref[...] = v
    vt_ref[...] = v.T.astype(BF16)

    gb_ref[...] = _dot(xn, w_ref[:, 3 * aw:4 * aw])
    u_ref[...] = _dot(xn, w_ref[:, 4 * aw:5 * aw]) * _dot(xn, w_ref[:, 5 * aw:6 * aw])


def _project(x, g, w_bf16, q_gain, k_gain, gsum, gexp):
    rows, d = x.shape
    pw = w_bf16.shape[1]
    aw = pw // 6
    tm = min(rows, 256)
    assert rows % tm == 0 and aw % LANES == 0
    row_blk = lambda width: pl.BlockSpec((tm, width), lambda i: (i, 0))
    col_blk = pl.BlockSpec((aw, tm), lambda i: (0, i))
    const = lambda shape: pl.BlockSpec(shape, lambda i: (0, 0))
    vmem = (d * pw * 2 + 2 * tm * d * 4 + 2 * tm * aw * (2 + 4 + 2 + 4 + 2 + 4 + 4)
            + 8 * tm * aw * 4 + 4 * aw * (2 * LANES + 2 * LANES))
    return pl.pallas_call(
        functools.partial(_proj_kernel, aw=aw),
        grid=(rows // tm,),
        in_specs=[row_blk(d), const((1, d)),
                  pl.BlockSpec((d, pw), lambda i: (0, 0), pipeline_mode=pl.Buffered(1)),
                  const((1, aw)), const((1, aw)),
                  const((2 * aw, LANES)), const((2 * LANES, aw))],
        out_specs=[col_blk, row_blk(aw), row_blk(aw), row_blk(aw), col_blk,
                   row_blk(aw), row_blk(aw)],
        out_shape=[jax.ShapeDtypeStruct((aw, rows), BF16),
                   jax.ShapeDtypeStruct((rows, aw), F32),
                   jax.ShapeDtypeStruct((rows, aw), BF16),
                   jax.ShapeDtypeStruct((rows, aw), F32),
                   jax.ShapeDtypeStruct((aw, rows), BF16),
                   jax.ShapeDtypeStruct((rows, aw), F32),
                   jax.ShapeDtypeStruct((rows, aw), F32)],
        compiler_params=pltpu.CompilerParams(
            dimension_semantics=("arbitrary",), vmem_limit_bytes=_vmem_limit(vmem)),
        name="in_proj",
    )(x, g, w_bf16, q_gain, k_gain, gsum, gexp)


def _lambda(lq_ref, lk_ref, lam_init):
    e = jnp.exp(jnp.sum(lq_ref[...] * lk_ref[...], axis=-1, keepdims=True))
    return e[0:1] - e[1:2] + lam_init


def _prompt_attn_kernel(qt_ref, k_ref, vt_ref, bias_ref, lq_ref, lk_ref, gs_ref,
                        o_ref, m_scr, l_scr, acc_scr, *, t, nd, lam_init):
    qi = pl.program_id(1)
    qt = qt_ref[...]
    row = lax.broadcasted_iota(jnp.int32, qt.shape, 0)
    zero = jnp.zeros_like(qt)
    qmaps = (jnp.where(row < QK_HEAD_DIM, qt, zero), jnp.where(row >= QK_HEAD_DIM, qt, zero))
    m_scr[...] = jnp.full(m_scr.shape, NEG_INF, F32)
    l_scr[...] = jnp.zeros(l_scr.shape, F32)
    acc_scr[...] = jnp.zeros(acc_scr.shape, F32)

    def body(ki, carry):
        start = pl.multiple_of(ki * t, t)
        kb = k_ref[pl.ds(start, t), :]
        vt = vt_ref[:, pl.ds(start, t)]
        b = bias_ref[0, jnp.minimum(qi - ki, nd)]
        for c in range(2):
            s = _dot(kb, qmaps[c]) + b
            m_prev = m_scr[c]
            m_new = jnp.maximum(m_prev, jnp.max(s, axis=0, keepdims=True))
            p = jnp.exp2(s - m_new)
            alpha = jnp.exp2(m_prev - m_new)
            l_scr[c] = alpha * l_scr[c] + jnp.sum(p, axis=0, keepdims=True)
            acc_scr[c] = alpha * acc_scr[c] + _dot(vt, p.astype(BF16))
            m_scr[c] = m_new
        return carry

    lax.fori_loop(0, qi + 1, body, 0)

    lam = _lambda(lq_ref, lk_ref, lam_init)
    o = acc_scr[0] / l_scr[0] - lam * (acc_scr[1] / l_scr[1])
    o = o * lax.rsqrt(jnp.mean(o * o, axis=0, keepdims=True) + RMS_EPS)
    o_ref[...] = (o.T * gs_ref[...] * (1.0 - lam_init)).astype(o_ref.dtype)


def _prompt_attention(qt, kb, vt, bias, lq, lk, g_sub, *, t, nd, lam_init):
    aw, s = qt.shape
    vmem = (2 * t * LANES * 2 + 2 * LANES * s * 2 + 2 * s * LANES * 2
            + 2 * (nd + 1) * t * t * 4 + 2 * t * LANES * 2
            + 2 * t * (2 * SUBLANES + LANES) * 4 + 8 * t * t * 4)
    return pl.pallas_call(
        functools.partial(_prompt_attn_kernel, t=t, nd=nd, lam_init=lam_init),
        grid=(N_HEADS, s // t),
        in_specs=[pl.BlockSpec((V_HEAD_DIM, t), lambda h, i: (h, i)),
                  pl.BlockSpec((s, V_HEAD_DIM), lambda h, i: (0, h)),
                  pl.BlockSpec((V_HEAD_DIM, s), lambda h, i: (h, 0)),
                  pl.BlockSpec((1, nd + 1, t, t), lambda h, i: (h, 0, 0, 0)),
                  pl.BlockSpec((2, QK_HEAD_DIM), lambda h, i: (0, 0)),
                  pl.BlockSpec((2, QK_HEAD_DIM), lambda h, i: (0, 0)),
                  pl.BlockSpec((1, V_HEAD_DIM), lambda h, i: (0, 0))],
        out_specs=pl.BlockSpec((t, V_HEAD_DIM), lambda h, i: (i, h)),
        out_shape=jax.ShapeDtypeStruct((s, aw), BF16),
        scratch_shapes=[pltpu.VMEM((2, 1, t), F32), pltpu.VMEM((2, 1, t), F32),
                        pltpu.VMEM((2, V_HEAD_DIM, t), F32)],
        compiler_params=pltpu.CompilerParams(
            dimension_semantics=("arbitrary", "arbitrary"),
            vmem_limit_bytes=_vmem_limit(vmem)),
        name="prompt_attn",
    )(qt, kb, vt, bias, lq, lk, g_sub)


def _sample_attn_kernel(pt_ref, qr_ref, kn_ref, vn_ref, bias_ref, bself_ref, lq_ref, lk_ref,
                        gs_ref, *rest, pages, page, lam_init):
    del pt_ref
    k_refs = rest[:pages]
    v_refs = rest[pages:2 * pages]
    o_ref, m_scr, l_scr, acc_scr = rest[2 * pages:]
    step = pl.program_id(1)
    qr = qr_ref[0]
    half = qr.shape[0] // 2

    def both_maps(b):
        return jnp.concatenate([b, b], axis=0)

    def update(s, vs):
        m_prev = m_scr[...]
        m_new = jnp.maximum(m_prev, jnp.max(s, axis=-1, keepdims=True))
        p = jnp.exp2(s - m_new)
        alpha = jnp.exp2(m_prev - m_new)
        l_scr[...] = alpha * l_scr[...] + jnp.sum(p, axis=-1, keepdims=True)
        p = p.astype(BF16)
        acc = alpha * acc_scr[...]
        for j, vj in enumerate(vs):
            acc = acc + _dot(p[:, j * page:(j + 1) * page], vj)
        acc_scr[...] = acc
        m_scr[...] = m_new

    @pl.when(step == 0)
    def _():
        m_scr[...] = jnp.full(m_scr.shape, NEG_INF, F32)
        l_scr[...] = jnp.zeros(l_scr.shape, F32)
        acc_scr[...] = jnp.zeros(acc_scr.shape, F32)
        update(_dot(qr, kn_ref[0]) + both_maps(bself_ref[...]), [vn_ref[0]])

    def v_page(ref):
        return jnp.concatenate(
            [ref[0, pl.ds(h, page, stride=N_HEADS), :] for h in range(N_HEADS)],
            axis=1).astype(BF16)

    s = jnp.concatenate(
        [_dot(qr, k_refs[j][0].astype(BF16)) + both_maps(bias_ref[j]) for j in range(pages)],
        axis=1)
    update(s, [v_page(v_refs[j]) for j in range(pages)])

    @pl.when(step == pl.num_programs(1) - 1)
    def _():
        o = acc_scr[...] / l_scr[...]
        lam = _lambda(lq_ref, lk_ref, lam_init)
        o = o[:half] - lam * o[half:]
        row_head = lax.broadcasted_iota(jnp.int32, o.shape, 0) % N_HEADS
        lane_head = lax.broadcasted_iota(jnp.int32, o.shape, 1) // V_HEAD_DIM
        o = jnp.where(row_head == lane_head, o, 0.0)
        ms = jnp.sum(o * o, axis=-1, keepdims=True) * (1.0 / V_HEAD_DIM)
        o = o * lax.rsqrt(ms + RMS_EPS)
        o = jnp.sum(o.reshape(half // N_HEADS, N_HEADS, o.shape[1]), axis=1)
        o_ref[0] = (o * gs_ref[...] * (1.0 - lam_init)).astype(o_ref.dtype)


def _sample_attention(page_table, q_rows, kt_new, v_new, bias_pages, bias_self, lq, lk, g_sub_t,
                      cache_kt, cache_vr, *, pages, lam_init):
    nb, n_pages = page_table.shape
    _, aw, page = cache_kt.shape
    dt = q_rows.shape[1] // (2 * N_HEADS)
    steps = n_pages // pages
    assert n_pages % pages == 0
    rows = q_rows.shape[1]

    def page_spec(shape):
        return [pl.BlockSpec((1,) + shape,
                             lambda b, s, pt, j=j: (pt[b * n_pages + s * pages + j], 0, 0))
                for j in range(pages)]

    per_b = lambda shape: pl.BlockSpec(shape, lambda b, s, pt: (b, 0, 0))
    const2 = lambda shape: pl.BlockSpec(shape, lambda b, s, pt: (0, 0))
    vmem = (2 * 2 * pages * page * aw * 4 + 3 * pages * page * aw * 2
            + 4 * rows * aw * 4 + 2 * 3 * rows * aw * 2)
    grid_spec = pltpu.PrefetchScalarGridSpec(
        num_scalar_prefetch=1,
        grid=(nb, steps),
        in_specs=[per_b((1, rows, aw)), per_b((1, aw, page)), per_b((1, page, aw)),
                  pl.BlockSpec((pages, rows // 2, page), lambda b, s, pt: (s, 0, 0)),
                  const2((rows // 2, page)),
                  const2((2, QK_HEAD_DIM)), const2((2, QK_HEAD_DIM)), const2((1, aw))]
                 + page_spec((aw, page)) + page_spec((page * N_HEADS, V_HEAD_DIM)),
        out_specs=pl.BlockSpec((1, dt, aw), lambda b, s, pt: (b, 0, 0)),
        scratch_shapes=[pltpu.VMEM((rows, 1), F32), pltpu.VMEM((rows, 1), F32),
                        pltpu.VMEM((rows, aw), F32)])
    return pl.pallas_call(
        functools.partial(_sample_attn_kernel, pages=pages, page=page, lam_init=lam_init),
        grid_spec=grid_spec,
        out_shape=jax.ShapeDtypeStruct((nb, dt, aw), BF16),
        compiler_params=pltpu.CompilerParams(
            dimension_semantics=("arbitrary", "arbitrary"),
            vmem_limit_bytes=_vmem_limit(vmem)),
        name="sample_attn",
    )(page_table.reshape(-1), q_rows, kt_new, v_new, bias_pages, bias_self, lq, lk, g_sub_t,
      *([cache_kt] * pages), *([cache_vr] * pages))


def _out_proj_kernel(*refs, halo):
    if halo:
        a_ref, gb_ref, u_ref, prev_ref, cw_ref, wo_ref, x_ref, o_ref = refs
        u = u_ref[...]
        first = pl.program_id(0) == 0
        prev = jnp.where(first, 0.0, prev_ref[...])
        ucat = jnp.concatenate([prev, u], axis=0)
        p1 = pltpu.roll(ucat, 1, axis=0)[SUBLANES:]
        p2 = pltpu.roll(ucat, 2, axis=0)[SUBLANES:]
    else:
        a_ref, gb_ref, u_ref, p1_ref, p2_ref, cw_ref, wo_ref, x_ref, o_ref = refs
        u, p1, p2 = u_ref[...], p1_ref[...], p2_ref[...]
    cw = cw_ref[...]
    aw = u.shape[1]
    c = gb_ref[...] * (p2 * cw[0:1] + p1 * cw[1:2] + u * cw[2:3])
    y = _dot(a_ref[...], wo_ref[:aw, :]) + _dot(c.astype(BF16), wo_ref[aw:, :])
    o_ref[...] = x_ref[...] + y


def _out_project(a, gb, u, conv_w, wo_bf16, x, prev=None):
    rows, d = x.shape
    aw = a.shape[1]
    tm = min(rows, 256)
    assert rows % tm == 0
    row_blk = lambda width: pl.BlockSpec((tm, width), lambda i: (i, 0))
    const = lambda shape: pl.BlockSpec(shape, lambda i: (0, 0))
    halo = prev is None
    if halo:
        nsub = tm // SUBLANES
        extra = [pl.BlockSpec((SUBLANES, aw), lambda i: (jnp.maximum(i * nsub - 1, 0), 0))]
        extra_args = [u]
    else:
        extra = [row_blk(aw), row_blk(aw)]
        extra_args = list(prev)
    vmem = 2 * d * d * 2 + 2 * tm * (aw * (2 + 4 + 4 + 8) + d * 8) + 6 * tm * d * 4
    return pl.pallas_call(
        functools.partial(_out_proj_kernel, halo=halo),
        grid=(rows // tm,),
        in_specs=[row_blk(aw), row_blk(aw), row_blk(aw)] + extra
                 + [const((CONV_K, aw)), const((d, d)), row_blk(d)],
        out_specs=row_blk(d),
        out_shape=jax.ShapeDtypeStruct((rows, d), F32),
        compiler_params=pltpu.CompilerParams(
            dimension_semantics=("arbitrary",), vmem_limit_bytes=_vmem_limit(vmem)),
        name="out_proj",
    )(a, gb, u, *extra_args, conv_w, wo_bf16, x)


ROUTER_ROWS = 32
EXPERT_ROW0 = 8


def _router_kernel(x_ref, g_ref, w3_ref, bias_ref, idx_ref, wt_ref):
    x = x_ref[...]
    ms = jnp.mean(x * x, axis=-1, keepdims=True)
    xn = x * lax.rsqrt(ms + RMS_EPS) * g_ref[...]
    hi, lo = _split_bf16(xn)
    logits = _dot_nt(w3_ref[...], jnp.concatenate([hi, hi, lo], axis=1)) + bias_ref[...]
    big = jnp.int32(1 << 20)

    gl = logits[0:SUBLANES]
    grow = lax.broadcasted_iota(jnp.int32, gl.shape, 0)
    gmax = jnp.max(gl, axis=0, keepdims=True)
    g_idx = jnp.min(jnp.where(gl == gmax, grow, big), axis=0, keepdims=True)
    g_w = 1.0 / jnp.sum(jnp.exp(gl - gmax), axis=0, keepdims=True)

    el = logits[EXPERT_ROW0:EXPERT_ROW0 + N_EXPERTS]
    erow = lax.broadcasted_iota(jnp.int32, el.shape, 0)
    el = jnp.where(erow // EXPERTS_PER_GROUP == g_idx, el, NEG_INF)
    v1 = jnp.max(el, axis=0, keepdims=True)
    i1 = jnp.min(jnp.where(el == v1, erow, big), axis=0, keepdims=True)
    el2 = jnp.where(erow == i1, NEG_INF, el)
    v2 = jnp.max(el2, axis=0, keepdims=True)
    i2 = jnp.min(jnp.where(el2 == v2, erow, big), axis=0, keepdims=True)
    t = jnp.exp(v2 - v1)
    w1 = g_w / (1.0 + t)
    idx_ref[...] = jnp.concatenate([i1, i2], axis=0)
    wt_ref[...] = jnp.concatenate([w1, w1 * t], axis=0)


def _route(x, g, w3, bias_col):
    rows, d = x.shape
    tm = math.gcd(rows, 256)
    assert tm % LANES == 0
    return pl.pallas_call(
        _router_kernel,
        grid=(rows // tm,),
        in_specs=[pl.BlockSpec((tm, d), lambda i: (i, 0)),
                  pl.BlockSpec((1, d), lambda i: (0, 0)),
                  pl.BlockSpec((ROUTER_ROWS, 3 * d), lambda i: (0, 0)),
                  pl.BlockSpec((ROUTER_ROWS, 1), lambda i: (0, 0))],
        out_specs=[pl.BlockSpec((2, tm), lambda i: (0, i)),
                   pl.BlockSpec((2, tm), lambda i: (0, i))],
        out_shape=[jax.ShapeDtypeStruct((2, rows), jnp.int32),
                   jax.ShapeDtypeStruct((2, rows), F32)],
        compiler_params=pltpu.CompilerParams(dimension_semantics=("arbitrary",)),
        name="router",
    )(x, g, w3, bias_col)


def _expert_kernel(te_ref, tv_ref, src_ref, dst_ref, roww_ref, g_ref, wg_ref, wu_ref, wd_ref,
                   x_hbm, y_hbm, xbuf, ybuf, gsem, ssem, *, tm):
    del te_ref
    i = pl.program_id(0)

    @pl.when(tv_ref[i] > 0)
    def _():
        for r in range(tm):
            pltpu.make_async_copy(x_hbm.at[pl.ds(src_ref[0, 0, r], 1)], xbuf.at[pl.ds(r, 1)],
                                  gsem).start()
        pltpu.make_async_copy(x_hbm.at[pl.ds(0, tm)], xbuf, gsem).wait()

        x = xbuf[...]
        ms = jnp.mean(x * x, axis=-1, keepdims=True)
        xn = (x * lax.rsqrt(ms + RMS_EPS) * g_ref[...]).astype(BF16)
        hg = _dot(xn, wg_ref[0].astype(BF16))
        hu = _dot(xn, wu_ref[0].astype(BF16))
        h = hg * jax.nn.sigmoid(hg) * hu * roww_ref[...]
        ybuf[...] = _dot(h.astype(BF16), wd_ref[0].astype(BF16))

    for r in range(tm):
        pltpu.make_async_copy(ybuf.at[pl.ds(r, 1)], y_hbm.at[pl.ds(dst_ref[0, 0, r], 1)],
                              ssem).start()
    pltpu.make_async_copy(ybuf, y_hbm.at[pl.ds(0, tm)], ssem).wait()


def _experts(tile_expert, tile_valid, row_src, row_dst, row_w, g, w_gate, w_up, w_down, x,
             *, tm, out_rows):
    nt = tile_expert.shape[0]
    d = x.shape[1]
    ff = w_gate.shape[2]
    smem_row = pl.BlockSpec((1, 1, tm), lambda i, te, tv: (i, 0, 0), memory_space=pltpu.SMEM)
    expert_w = lambda shape: pl.BlockSpec((1,) + shape, lambda i, te, tv: (te[i], 0, 0))
    grid_spec = pltpu.PrefetchScalarGridSpec(
        num_scalar_prefetch=2,
        grid=(nt,),
        in_specs=[smem_row, smem_row,
                  pl.BlockSpec((tm, 1), lambda i, te, tv: (i, 0)),
                  pl.BlockSpec((1, d), lambda i, te, tv: (0, 0)),
                  expert_w((d, ff)), expert_w((d, ff)), expert_w((ff, d)),
                  pl.BlockSpec(memory_space=pl.ANY)],
        out_specs=pl.BlockSpec(memory_space=pl.ANY),
        scratch_shapes=[pltpu.VMEM((tm, d), F32), pltpu.VMEM((tm, d), F32),
                        pltpu.SemaphoreType.DMA(()), pltpu.SemaphoreType.DMA(())])
    vmem = 2 * 3 * d * ff * 4 + 3 * d * ff * 2 + 2 * tm * d * 4 + 6 * tm * d * 4
    return pl.pallas_call(
        functools.partial(_expert_kernel, tm=tm),
        grid_spec=grid_spec,
        out_shape=jax.ShapeDtypeStruct((out_rows, d), F32),
        compiler_params=pltpu.CompilerParams(
            dimension_semantics=("arbitrary",), vmem_limit_bytes=_vmem_limit(vmem)),
        name="experts",
    )(tile_expert, tile_valid, row_src, row_dst, row_w, g, w_gate, w_up, w_down, x)


def _combine_kernel(x_ref, y0_ref, y1_ref, o_ref):
    o_ref[...] = x_ref[...] + (y0_ref[...] + y1_ref[...])


def _combine(x2, y2, row0, rows, total_rows):
    d = x2.shape[1]
    tm = math.gcd(math.gcd(rows, row0) if row0 else rows, LANES)
    assert total_rows % tm == 0 and tm % SUBLANES == 0
    b0 = row0 // tm
    b1 = (total_rows + row0) // tm
    return pl.pallas_call(
        _combine_kernel,
        grid=(rows // tm,),
        in_specs=[pl.BlockSpec((tm, d), lambda i: (b0 + i, 0)),
                  pl.BlockSpec((tm, d), lambda i: (b0 + i, 0)),
                  pl.BlockSpec((tm, d), lambda i: (b1 + i, 0))],
        out_specs=pl.BlockSpec((tm, d), lambda i: (i, 0)),
        out_shape=jax.ShapeDtypeStruct((rows, d), F32),
        compiler_params=pltpu.CompilerParams(dimension_semantics=("arbitrary",)),
        name="moe_combine",
    )(x2, y2, y2)


def _routing_plan(e_idx, w, tm):
    total = e_idx.shape[1]
    n_assign = 2 * total
    nt = -(-n_assign // tm) + N_EXPERTS
    e_flat = e_idx.reshape(-1)
    w_flat = w.reshape(-1)
    order = jnp.argsort(e_flat, stable=True).astype(jnp.int32)
    counts = jnp.sum((e_flat[:, None] == jnp.arange(N_EXPERTS, dtype=jnp.int32)[None, :])
                     .astype(jnp.int32), axis=0)
    tiles_per = (counts + tm - 1) // tm
    tile_end = jnp.cumsum(tiles_per)
    tile_start = tile_end - tiles_per
    group_start = jnp.cumsum(counts) - counts
    tiles = jnp.arange(nt, dtype=jnp.int32)
    tile_expert = jnp.minimum(
        jnp.sum((tiles[:, None] >= tile_end[None, :]).astype(jnp.int32), axis=1), N_EXPERTS - 1)
    tile_valid = (tiles < tile_end[-1]).astype(jnp.int32)
    rank = (tiles - tile_start[tile_expert])[:, None] * tm + jnp.arange(tm, dtype=jnp.int32)[None, :]
    valid = (rank < counts[tile_expert][:, None]) & (tile_valid[:, None] > 0)
    pos = jnp.clip(group_start[tile_expert][:, None] + rank, 0, n_assign - 1)
    assign = order[pos]
    pad_rank = jnp.cumsum(jnp.logical_not(valid).reshape(-1).astype(jnp.int32)) - 1
    dump = n_assign + pad_rank.reshape(nt, tm)
    row_src = jnp.where(valid, assign % total, 0)
    row_dst = jnp.where(valid, assign, dump)
    row_w = jnp.where(valid, w_flat[assign], 0.0)
    return (tile_expert, tile_valid, row_src.reshape(nt, 1, tm), row_dst.reshape(nt, 1, tm),
            row_w.reshape(nt * tm, 1)), nt * tm


def _group_matrices(aw):
    e = jnp.arange(aw, dtype=jnp.int32) // QK_HEAD_DIM
    lanes = jnp.arange(LANES, dtype=jnp.int32)
    member = (e[:, None] == lanes[None, :]).astype(BF16)
    return jnp.concatenate([member, member], axis=0), jnp.concatenate([member.T, member.T], axis=0)


def _prompt_bias(rel_table, t, nd):
    i = jnp.arange(t, dtype=jnp.int32)
    dist = (jnp.arange(nd + 1, dtype=jnp.int32)[:, None, None] * t + i[None, None, :]
            - i[None, :, None])
    bias = _bias_table(rel_table, dist.reshape(-1, LANES))
    return bias.reshape(N_HEADS, nd + 1, t, t)


def _sample_bias(rel_table, past, n_pages, page, dt):
    tq = jnp.arange(dt, dtype=jnp.int32)
    kpos = jnp.arange(n_pages * page, dtype=jnp.int32).reshape(n_pages, 1, page)
    dist_pages = (past + tq[None, :, None] - kpos).reshape(-1, page)
    j = jnp.arange(page, dtype=jnp.int32)
    self_rows = jnp.where(j[None, :] < dt, tq[:, None] - j[None, :], -1)
    dist = jnp.concatenate([dist_pages, self_rows], axis=0)
    rows = dist.shape[0]
    rpad = (-rows) % (256 if rows > 256 else SUBLANES)
    dist = jnp.concatenate([dist, jnp.full((rpad, page), -1, jnp.int32)], axis=0)
    bias = _bias_table(rel_table, dist)
    bp = bias[:, :n_pages * dt].reshape(N_HEADS, n_pages, dt, page)
    bp = bp.transpose(1, 2, 0, 3).reshape(n_pages, dt * N_HEADS, page)
    bs = bias[:, n_pages * dt:n_pages * dt + dt].transpose(1, 0, 2).reshape(dt * N_HEADS, page)
    return bp, bs


def kernel(x_prompt, x_sample, cache_k, cache_v, state_conv, page_table, rel_table, attn_norm_g,
           w_in, q_norm_g, k_norm_g, lambda_q, lambda_k, sub_norm_g, conv_w, w_o, ffn_norm_g,
           router_group_w, router_group_b, router_expert_w, router_expert_b, w_gate, w_up, w_down):
    depth = w_in.shape[0]
    assert depth == 1, "single-layer trunk"
    batch, seq, d = x_prompt.shape
    assert batch == 1, "one prompt sequence"
    nb, dt, _ = x_sample.shape
    n_pool, page = cache_k.shape[1], cache_k.shape[2]
    n_pages = page_table.shape[1]
    past = n_pages * page
    aw = N_HEADS * V_HEAD_DIM
    layer = 0
    lam_init = 0.8 - 0.6 * math.exp(-0.3 * layer)

    w_in_b = w_in[layer].astype(BF16)
    w_o_b = w_o[layer].astype(BF16)
    g_attn = attn_norm_g[layer].reshape(1, d)
    g_ffn = ffn_norm_g[layer].reshape(1, d)
    qg = jnp.tile(q_norm_g[layer], aw // QK_HEAD_DIM).reshape(1, aw)
    kg = jnp.tile(k_norm_g[layer], aw // QK_HEAD_DIM).reshape(1, aw)
    g_sub = sub_norm_g[layer].reshape(1, V_HEAD_DIM)
    g_sub_t = jnp.tile(sub_norm_g[layer], N_HEADS).reshape(1, aw)
    gsum, gexp = _group_matrices(aw)
    lq, lk = lambda_q[layer], lambda_k[layer]
    cw = conv_w[layer]

    xp = x_prompt.reshape(seq, d)
    xs = x_sample.reshape(nb * dt, d)

    qtp, kp, kbp, vp, vtp, gbp, up = _project(xp, g_attn, w_in_b, qg, kg, gsum, gexp)
    qts, ks, _, vs, _, gbs, us = _project(xs, g_attn, w_in_b, qg, kg, gsum, gexp)

    t = min(seq, 512)
    assert seq % t == 0
    nd = min(-(-(MAX_DISTANCE + t - 1) // t), seq // t)
    bias_p = _prompt_bias(rel_table, t, nd)
    ap = _prompt_attention(qtp, kbp, vtp, bias_p, lq, lk, g_sub, t=t, nd=nd, lam_init=lam_init)

    bias_pages, bias_self = _sample_bias(rel_table, past, n_pages, page, dt)
    lane_hc = jnp.arange(aw, dtype=jnp.int32) // QK_HEAD_DIM
    row_c = jnp.arange(2, dtype=jnp.int32)[:, None, None]
    row_h = jnp.arange(N_HEADS, dtype=jnp.int32)[None, None, :]
    sel = (lane_hc[None, None, None, :] == (row_h * 2 + row_c)[..., None])
    q3 = qts.T.reshape(nb, 1, dt, 1, aw)
    q_rows = jnp.where(sel[None], q3, jnp.zeros((), BF16)).reshape(nb, 2 * dt * N_HEADS, aw)
    k3 = ks.reshape(nb, dt, aw).astype(BF16)
    kt_new = jnp.pad(k3.transpose(0, 2, 1), ((0, 0), (0, 0), (0, page - dt)))
    v_new = jnp.pad(vs.reshape(nb, dt, aw).astype(BF16), ((0, 0), (0, page - dt), (0, 0)))
    cache_kt = cache_k[layer].transpose(0, 2, 3, 4, 1).reshape(n_pool, aw, page)
    cache_vr = cache_v[layer].reshape(n_pool, page * N_HEADS, V_HEAD_DIM)
    pages_per_step = math.gcd(n_pages, 8)
    a_s = _sample_attention(page_table, q_rows, kt_new, v_new, bias_pages, bias_self,
                            lq, lk, g_sub_t, cache_kt, cache_vr,
                            pages=pages_per_step, lam_init=lam_init)
    a_s = a_s.reshape(nb * dt, aw)

    x2p = _out_project(ap, gbp, up, cw, w_o_b, xp)
    u_ext = jnp.concatenate([state_conv[layer].astype(F32), us.reshape(nb, dt, aw)], axis=1)
    prev1 = u_ext[:, 1:1 + dt].reshape(nb * dt, aw)
    prev2 = u_ext[:, 0:dt].reshape(nb * dt, aw)
    x2s = _out_project(a_s, gbs, us, cw, w_o_b, xs, prev=(prev1, prev2))
    x2 = jnp.concatenate([x2p, x2s], axis=0)
    total = seq + nb * dt

    wr = jnp.zeros((ROUTER_ROWS, d), F32)
    wr = wr.at[0:N_GROUPS].set(router_group_w[layer].T)
    wr = wr.at[EXPERT_ROW0:EXPERT_ROW0 + N_EXPERTS].set(router_expert_w[layer].T)
    wr_hi, wr_lo = _split_bf16(wr)
    w3 = jnp.concatenate([wr_hi, wr_lo, wr_hi], axis=1)
    bias_col = jnp.full((ROUTER_ROWS,), NEG_INF, F32)
    bias_col = bias_col.at[0:N_GROUPS].set(router_group_b[layer].astype(F32))
    bias_col = bias_col.at[EXPERT_ROW0:EXPERT_ROW0 + N_EXPERTS].set(
        router_expert_b[layer].reshape(-1).astype(F32)).reshape(ROUTER_ROWS, 1)
    e_idx, wts = _route(x2, g_ffn, w3, bias_col)
    tm_e = 256
    plan, y_rows = _routing_plan(e_idx, wts, tm_e)
    y2 = _experts(*plan, g_ffn, w_gate[layer], w_up[layer], w_down[layer], x2,
                  tm=tm_e, out_rows=y_rows)
    yp = _combine(x2, y2, 0, seq, total)
    ys = _combine(x2, y2, seq, nb * dt, total)

    return (yp.reshape(batch, seq, d),
            ys.reshape(nb, dt, d),
            kp.reshape(depth, batch, seq, N_HEADS, 2, QK_HEAD_DIM),
            vp.reshape(depth, batch, seq, N_HEADS, V_HEAD_DIM),
            up[seq - (CONV_K - 1):].reshape(depth, batch, CONV_K - 1, aw),
            ks.reshape(depth, nb, dt, N_HEADS, 2, QK_HEAD_DIM),
            vs.reshape(depth, nb, dt, N_HEADS, V_HEAD_DIM),
            u_ext[:, dt:].reshape(depth, nb, CONV_K - 1, aw))
```

```python
import functools
import math

import jax
import jax.numpy as jnp
from jax import lax
from jax.experimental import pallas as pl
from jax.experimental.pallas import tpu as pltpu

N_HEADS = 8
V_HEAD_DIM = 128
QK_HEAD_DIM = 64
CONV_K = 3
N_BUCKETS = 32
MAX_EXACT = N_BUCKETS // 2
MAX_DISTANCE = 2048
N_GROUPS = 4
EXPERTS_PER_GROUP = 4
N_EXPERTS = N_GROUPS * EXPERTS_PER_GROUP
RMS_EPS = 1e-6
NEG_INF = -1e30
LOG2E = math.log2(math.e)

LANES = 128
SUBLANES = 8
VMEM_PHYSICAL_BYTES = 64 * 1024 * 1024

F32 = jnp.float32
BF16 = jnp.bfloat16


def _vmem_limit(nbytes):
    return int(min(max(nbytes * 5 // 4 + (4 << 20), 16 << 20), VMEM_PHYSICAL_BYTES - (6 << 20)))


def _dot(a, b):
    return jnp.dot(a, b, preferred_element_type=F32)


def _dot_nt(a, b):
    return lax.dot_general(a, b, (((1,), (1,)), ((), ())), preferred_element_type=F32)


def _split_bf16(x):
    hi = x.astype(BF16)
    lo = (x - hi.astype(F32)).astype(BF16)
    return hi, lo


def _bias_kernel(table_ref, dist_ref, out_ref):
    d = dist_ref[...]
    n = jnp.maximum(d, 0)
    nf = jnp.maximum(n, 1).astype(F32)
    large = MAX_EXACT + (jnp.log(nf / MAX_EXACT) / math.log(MAX_DISTANCE / MAX_EXACT)
                         * (N_BUCKETS - MAX_EXACT)).astype(jnp.int32)
    large = jnp.minimum(large, N_BUCKETS - 1)
    bucket = jnp.where(n < MAX_EXACT, n, large)
    hits = [bucket == b for b in range(N_BUCKETS)]
    for h in range(N_HEADS):
        acc = jnp.zeros(d.shape, F32)
        for b in range(N_BUCKETS):
            acc = jnp.where(hits[b], table_ref[b, h] * LOG2E, acc)
        out_ref[h] = jnp.where(d < 0, NEG_INF, acc)


def _bias_table(rel_table, dist):
    rows = dist.shape[0]
    tr = min(rows, 256)
    assert rows % tr == 0
    return pl.pallas_call(
        _bias_kernel,
        grid=(rows // tr,),
        in_specs=[pl.BlockSpec(memory_space=pltpu.SMEM),
                  pl.BlockSpec((tr, LANES), lambda i: (i, 0))],
        out_specs=pl.BlockSpec((N_HEADS, tr, LANES), lambda i: (0, i, 0)),
        out_shape=jax.ShapeDtypeStruct((N_HEADS, rows, LANES), F32),
        name="rel_bias",
    )(rel_table.astype(F32), dist)


def _group_rms_scale(z, gsum, gexp):
    hi, lo = _split_bf16(z * z)
    ssq = _dot(jnp.concatenate([hi, lo], axis=1), gsum)
    r = lax.rsqrt(ssq * (1.0 / QK_HEAD_DIM) + RMS_EPS)
    r_hi, r_lo = _split_bf16(r)
    return _dot(jnp.concatenate([r_hi, r_lo], axis=1), gexp)


def _proj_kernel(x_ref, g_ref, w_ref, qg_ref, kg_ref, gsum_ref, gexp_ref,
                 qt_ref, kf_ref, kb_ref, vf_ref, vt_ref, gb_ref, u_ref, *, aw):
    x = x_ref[...]
    ms = jnp.mean(x * x, axis=-1, keepdims=True)
    xn = (x * lax.rsqrt(ms + RMS_EPS) * g_ref[...]).astype(BF16)
    gsum = gsum_ref[...]
    gexp = gexp_ref[...]

    zq = _dot(xn, w_ref[:, 0 * aw:1 * aw])
    q = zq * _group_rms_scale(zq, gsum, gexp) * qg_ref[...]
    qt_ref[...] = (q * (QK_HEAD_DIM ** -0.5 * LOG2E)).T.astype(BF16)

    zk = _dot(xn, w_ref[:, 1 * aw:2 * aw])
    k = zk * _group_rms_scale(zk, gsum, gexp) * kg_ref[...]
    kf_ref[...] = k
    kb_ref[...] = k.astype(BF16)

    v = _dot(xn, w_ref[:, 2 * aw:3 * aw])
    vf_ref[...] = v
    vt_ref[...] = v.T.astype(BF16)

    gb_ref[...] = _dot(xn, w_ref[:, 3 * aw:4 * aw])
    u_ref[...] = _dot(xn, w_ref[:, 4 * aw:5 * aw]) * _dot(xn, w_ref[:, 5 * aw:6 * aw])


def _project(x, g, w_bf16, q_gain, k_gain, gsum, gexp):
    rows, d = x.shape
    pw = w_bf16.shape[1]
    aw = pw // 6
    tm = min(rows, 256)
    assert rows % tm == 0 and aw % LANES == 0
    row_blk = lambda width: pl.BlockSpec((tm, width), lambda i: (i, 0))
    col_blk = pl.BlockSpec((aw, tm), lambda i: (0, i))
    const = lambda shape: pl.BlockSpec(shape, lambda i: (0, 0))
    vmem = (d * pw * 2 + 2 * tm * d * 4 + 2 * tm * aw * (2 + 4 + 2 + 4 + 2 + 4 + 4)
            + 8 * tm * aw * 4 + 4 * aw * (2 * LANES + 2 * LANES))
    return pl.pallas_call(
        functools.partial(_proj_kernel, aw=aw),
        grid=(rows // tm,),
        in_specs=[row_blk(d), const((1, d)),
                  pl.BlockSpec((d, pw), lambda i: (0, 0), pipeline_mode=pl.Buffered(1)),
                  const((1, aw)), const((1, aw)),
                  const((2 * aw, LANES)), const((2 * LANES, aw))],
        out_specs=[col_blk, row_blk(aw), row_blk(aw), row_blk(aw), col_blk,
                   row_blk(aw), row_blk(aw)],
        out_shape=[jax.ShapeDtypeStruct((aw, rows), BF16),
                   jax.ShapeDtypeStruct((rows, aw), F32),
                   jax.ShapeDtypeStruct((rows, aw), BF16),
                   jax.ShapeDtypeStruct((rows, aw), F32),
                   jax.ShapeDtypeStruct((aw, rows), BF16),
                   jax.ShapeDtypeStruct((rows, aw), F32),
                   jax.ShapeDtypeStruct((rows, aw), F32)],
        compiler_params=pltpu.CompilerParams(
            dimension_semantics=("arbitrary",), vmem_limit_bytes=_vmem_limit(vmem)),
        name="in_proj",
    )(x, g, w_bf16, q_gain, k_gain, gsum, gexp)


def _lambda(lq_ref, lk_ref, lam_init):
    e = jnp.exp(jnp.sum(lq_ref[...] * lk_ref[...], axis=-1, keepdims=True))
    return e[0:1] - e[1:2] + lam_init


def _prompt_attn_kernel(qt_ref, k_ref, vt_ref, bias_ref, lq_ref, lk_ref, gs_ref,
                        o_ref, m_scr, l_scr, acc_scr, sa_scr, sb_scr, *, t, nd, lam_init):
    qi = pl.program_id(1)
    qt = qt_ref[...]
    row = lax.broadcasted_iota(jnp.int32, qt.shape, 0)
    zero = jnp.zeros_like(qt)
    qmaps = (jnp.where(row < QK_HEAD_DIM, qt, zero), jnp.where(row >= QK_HEAD_DIM, qt, zero))
    m_scr[...] = jnp.full(m_scr.shape, NEG_INF, F32)
    l_scr[...] = jnp.zeros(l_scr.shape, F32)
    acc_scr[...] = jnp.zeros(acc_scr.shape, F32)

    def scores(ki, s_scr):
        kb = k_ref[pl.ds(pl.multiple_of(ki * t, t), t), :]
        for c in range(2):
            s_scr[c] = _dot(kb, qmaps[c])

    def attend(ki, s_scr, delta):
        vt = vt_ref[:, pl.ds(pl.multiple_of(ki * t, t), t)]
        b = bias_ref[0, delta]
        for c in range(2):
            s = s_scr[c] + b
            m_prev = m_scr[c]
            m_new = jnp.maximum(m_prev, jnp.max(s, axis=0, keepdims=True))
            p = jnp.exp2(s - m_new)
            alpha = jnp.exp2(m_prev - m_new)
            l_scr[c] = alpha * l_scr[c] + jnp.sum(p, axis=0, keepdims=True)
            acc_scr[c] = alpha * acc_scr[c] + _dot(vt, p.astype(BF16))
            m_scr[c] = m_new

    scores(0, sa_scr)

    def body(j, carry):
        k0 = 2 * j
        scores(k0 + 1, sb_scr)
        attend(k0, sa_scr, jnp.minimum(qi - k0, nd))
        scores(k0 + 2, sa_scr)
        attend(k0 + 1, sb_scr, jnp.minimum(qi - k0 - 1, nd))
        return carry

    lax.fori_loop(0, qi // 2, body, 0)

    @pl.when(qi % 2 == 0)
    def _():
        attend(qi, sa_scr, 0)

    @pl.when(qi % 2 == 1)
    def _():
        scores(qi, sb_scr)
        attend(qi - 1, sa_scr, min(1, nd))
        attend(qi, sb_scr, 0)

    lam = _lambda(lq_ref, lk_ref, lam_init)
    o = acc_scr[0] / l_scr[0] - lam * (acc_scr[1] / l_scr[1])
    o = o * lax.rsqrt(jnp.mean(o * o, axis=0, keepdims=True) + RMS_EPS)
    o_ref[...] = (o.T * gs_ref[...] * (1.0 - lam_init)).astype(o_ref.dtype)


def _prompt_attention(qt, kb, vt, bias, lq, lk, g_sub, *, t, nd, lam_init):
    aw, s = qt.shape
    vmem = (2 * t * LANES * 2 + 2 * LANES * s * 2 + 2 * s * LANES * 2
            + 2 * (nd + 1) * t * t * 4 + 2 * t * LANES * 2
            + 2 * t * (2 * SUBLANES + LANES) * 4 + 4 * t * t * 4 + 8 * t * t * 4)
    return pl.pallas_call(
        functools.partial(_prompt_attn_kernel, t=t, nd=nd, lam_init=lam_init),
        grid=(N_HEADS, s // t),
        in_specs=[pl.BlockSpec((V_HEAD_DIM, t), lambda h, i: (h, i)),
                  pl.BlockSpec((s, V_HEAD_DIM), lambda h, i: (0, h)),
                  pl.BlockSpec((V_HEAD_DIM, s), lambda h, i: (h, 0)),
                  pl.BlockSpec((1, nd + 1, t, t), lambda h, i: (h, 0, 0, 0)),
                  pl.BlockSpec((2, QK_HEAD_DIM), lambda h, i: (0, 0)),
                  pl.BlockSpec((2, QK_HEAD_DIM), lambda h, i: (0, 0)),
                  pl.BlockSpec((1, V_HEAD_DIM), lambda h, i: (0, 0))],
        out_specs=pl.BlockSpec((t, V_HEAD_DIM), lambda h, i: (i, h)),
        out_shape=jax.ShapeDtypeStruct((s, aw), BF16),
        scratch_shapes=[pltpu.VMEM((2, 1, t), F32), pltpu.VMEM((2, 1, t), F32),
                        pltpu.VMEM((2, V_HEAD_DIM, t), F32),
                        pltpu.VMEM((2, t, t), F32), pltpu.VMEM((2, t, t), F32)],
        compiler_params=pltpu.CompilerParams(
            dimension_semantics=("arbitrary", "arbitrary"),
            vmem_limit_bytes=_vmem_limit(vmem)),
        name="prompt_attn",
    )(qt, kb, vt, bias, lq, lk, g_sub)


def _sample_attn_kernel(pt_ref, qr_ref, kn_ref, vn_ref, bias_ref, bself_ref, lq_ref, lk_ref,
                        gs_ref, *rest, pages, page, lam_init):
    del pt_ref
    k_refs = rest[:pages]
    v_refs = rest[pages:2 * pages]
    o_ref, m_scr, l_scr, acc_scr = rest[2 * pages:]
    step = pl.program_id(1)
    qr = qr_ref[0]
    half = qr.shape[0] // 2

    def both_maps(b):
        return jnp.concatenate([b, b], axis=0)

    def update(s, vs):
        m_prev = m_scr[...]
        m_new = jnp.maximum(m_prev, jnp.max(s, axis=-1, keepdims=True))
        p = jnp.exp2(s - m_new)
        alpha = jnp.exp2(m_prev - m_new)
        l_scr[...] = alpha * l_scr[...] + jnp.sum(p, axis=-1, keepdims=True)
        p = p.astype(BF16)
        acc = alpha * acc_scr[...]
        for j, vj in enumerate(vs):
            acc = acc + _dot(p[:, j * page:(j + 1) * page], vj)
        acc_scr[...] = acc
        m_scr[...] = m_new

    @pl.when(step == 0)
    def _():
        m_scr[...] = jnp.full(m_scr.shape, NEG_INF, F32)
        l_scr[...] = jnp.zeros(l_scr.shape, F32)
        acc_scr[...] = jnp.zeros(acc_scr.shape, F32)
        update(_dot(qr, kn_ref[0]) + both_maps(bself_ref[...]), [vn_ref[0]])

    def v_page(ref):
        return jnp.concatenate(
            [ref[0, pl.ds(h, page, stride=N_HEADS), :] for h in range(N_HEADS)],
            axis=1).astype(BF16)

    s = jnp.concatenate(
        [_dot(qr, k_refs[j][0].astype(BF16)) + both_maps(bias_ref[j]) for j in range(pages)],
        axis=1)
    update(s, [v_page(v_refs[j]) for j in range(pages)])

    @pl.when(step == pl.num_programs(1) - 1)
    def _():
        o = acc_scr[...] / l_scr[...]
        lam = _lambda(lq_ref, lk_ref, lam_init)
        o = o[:half] - lam * o[half:]
        row_head = lax.broadcasted_iota(jnp.int32, o.shape, 0) % N_HEADS
        lane_head = lax.broadcasted_iota(jnp.int32, o.shape, 1) // V_HEAD_DIM
        o = jnp.where(row_head == lane_head, o, 0.0)
        ms = jnp.sum(o * o, axis=-1, keepdims=True) * (1.0 / V_HEAD_DIM)
        o = o * lax.rsqrt(ms + RMS_EPS)
        o = jnp.sum(o.reshape(half // N_HEADS, N_HEADS, o.shape[1]), axis=1)
        o_ref[0] = (o * gs_ref[...] * (1.0 - lam_init)).astype(o_ref.dtype)


def _sample_attention(page_table, q_rows, kt_new, v_new, bias_pages, bias_self, lq, lk, g_sub_t,
                      cache_kt, cache_vr, *, pages, lam_init):
    nb, n_pages = page_table.shape
    _, aw, page = cache_kt.shape
    dt = q_rows.shape[1] // (2 * N_HEADS)
    steps = n_pages // pages
    assert n_pages % pages == 0
    rows = q_rows.shape[1]

    def page_spec(shape):
        return [pl.BlockSpec((1,) + shape,
                             lambda b, s, pt, j=j: (pt[b * n_pages + s * pages + j], 0, 0))
                for j in range(pages)]

    per_b = lambda shape: pl.BlockSpec(shape, lambda b, s, pt: (b, 0, 0))
    const2 = lambda shape: pl.BlockSpec(shape, lambda b, s, pt: (0, 0))
    vmem = (2 * 2 * pages * page * aw * 4 + 3 * pages * page * aw * 2
            + 4 * rows * aw * 4 + 2 * 3 * rows * aw * 2)
    grid_spec = pltpu.PrefetchScalarGridSpec(
        num_scalar_prefetch=1,
        grid=(nb, steps),
        in_specs=[per_b((1, rows, aw)), per_b((1, aw, page)), per_b((1, page, aw)),
                  pl.BlockSpec((pages, rows // 2, page), lambda b, s, pt: (s, 0, 0)),
                  const2((rows // 2, page)),
                  const2((2, QK_HEAD_DIM)), const2((2, QK_HEAD_DIM)), const2((1, aw))]
                 + page_spec((aw, page)) + page_spec((page * N_HEADS, V_HEAD_DIM)),
        out_specs=pl.BlockSpec((1, dt, aw), lambda b, s, pt: (b, 0, 0)),
        scratch_shapes=[pltpu.VMEM((rows, 1), F32), pltpu.VMEM((rows, 1), F32),
                        pltpu.VMEM((rows, aw), F32)])
    return pl.pallas_call(
        functools.partial(_sample_attn_kernel, pages=pages, page=page, lam_init=lam_init),
        grid_spec=grid_spec,
        out_shape=jax.ShapeDtypeStruct((nb, dt, aw), BF16),
        compiler_params=pltpu.CompilerParams(
            dimension_semantics=("arbitrary", "arbitrary"),
            vmem_limit_bytes=_vmem_limit(vmem)),
        name="sample_attn",
    )(page_table.reshape(-1), q_rows, kt_new, v_new, bias_pages, bias_self, lq, lk, g_sub_t,
      *([cache_kt] * pages), *([cache_vr] * pages))


def _out_proj_kernel(*refs, halo):
    if halo:
        a_ref, gb_ref, u_ref, prev_ref, cw_ref, wo_ref, x_ref, o_ref = refs
        u = u_ref[...]
        first = pl.program_id(0) == 0
        prev = jnp.where(first, 0.0, prev_ref[...])
        ucat = jnp.concatenate([prev, u], axis=0)
        p1 = pltpu.roll(ucat, 1, axis=0)[SUBLANES:]
        p2 = pltpu.roll(ucat, 2, axis=0)[SUBLANES:]
    else:
        a_ref, gb_ref, u_ref, p1_ref, p2_ref, cw_ref, wo_ref, x_ref, o_ref = refs
        u, p1, p2 = u_ref[...], p1_ref[...], p2_ref[...]
    cw = cw_ref[...]
    aw = u.shape[1]
    c = gb_ref[...] * (p2 * cw[0:1] + p1 * cw[1:2] + u * cw[2:3])
    y = _dot(a_ref[...], wo_ref[:aw, :]) + _dot(c.astype(BF16), wo_ref[aw:, :])
    o_ref[...] = x_ref[...] + y


def _out_project(a, gb, u, conv_w, wo_bf16, x, prev=None):
    rows, d = x.shape
    aw = a.shape[1]
    tm = min(rows, 256)
    assert rows % tm == 0
    row_blk = lambda width: pl.BlockSpec((tm, width), lambda i: (i, 0))
    const = lambda shape: pl.BlockSpec(shape, lambda i: (0, 0))
    halo = prev is None
    if halo:
        nsub = tm // SUBLANES
        extra = [pl.BlockSpec((SUBLANES, aw), lambda i: (jnp.maximum(i * nsub - 1, 0), 0))]
        extra_args = [u]
    else:
        extra = [row_blk(aw), row_blk(aw)]
        extra_args = list(prev)
    vmem = 2 * d * d * 2 + 2 * tm * (aw * (2 + 4 + 4 + 8) + d * 8) + 6 * tm * d * 4
    return pl.pallas_call(
        functools.partial(_out_proj_kernel, halo=halo),
        grid=(rows // tm,),
        in_specs=[row_blk(aw), row_blk(aw), row_blk(aw)] + extra
                 + [const((CONV_K, aw)), const((d, d)), row_blk(d)],
        out_specs=row_blk(d),
        out_shape=jax.ShapeDtypeStruct((rows, d), F32),
        compiler_params=pltpu.CompilerParams(
            dimension_semantics=("arbitrary",), vmem_limit_bytes=_vmem_limit(vmem)),
        name="out_proj",
    )(a, gb, u, *extra_args, conv_w, wo_bf16, x)


ROUTER_ROWS = 32
EXPERT_ROW0 = 8


def _router_kernel(x_ref, g_ref, w3_ref, bias_ref, idx_ref, wt_ref):
    x = x_ref[...]
    ms = jnp.mean(x * x, axis=-1, keepdims=True)
    xn = x * lax.rsqrt(ms + RMS_EPS) * g_ref[...]
    hi, lo = _split_bf16(xn)
    logits = _dot_nt(w3_ref[...], jnp.concatenate([hi, hi, lo], axis=1)) + bias_ref[...]
    big = jnp.int32(1 << 20)

    gl = logits[0:SUBLANES]
    grow = lax.broadcasted_iota(jnp.int32, gl.shape, 0)
    gmax = jnp.max(gl, axis=0, keepdims=True)
    g_idx = jnp.min(jnp.where(gl == gmax, grow, big), axis=0, keepdims=True)
    g_w = 1.0 / jnp.sum(jnp.exp(gl - gmax), axis=0, keepdims=True)

    el = logits[EXPERT_ROW0:EXPERT_ROW0 + N_EXPERTS]
    erow = lax.broadcasted_iota(jnp.int32, el.shape, 0)
    el = jnp.where(erow // EXPERTS_PER_GROUP == g_idx, el, NEG_INF)
    v1 = jnp.max(el, axis=0, keepdims=True)
    i1 = jnp.min(jnp.where(el == v1, erow, big), axis=0, keepdims=True)
    el2 = jnp.where(erow == i1, NEG_INF, el)
    v2 = jnp.max(el2, axis=0, keepdims=True)
    i2 = jnp.min(jnp.where(el2 == v2, erow, big), axis=0, keepdims=True)
    t = jnp.exp(v2 - v1)
    w1 = g_w / (1.0 + t)
    idx_ref[...] = jnp.concatenate([i1, i2], axis=0)
    wt_ref[...] = jnp.concatenate([w1, w1 * t], axis=0)


def _route(x, g, w3, bias_col):
    rows, d = x.shape
    tm = math.gcd(rows, 256)
    assert tm % LANES == 0
    return pl.pallas_call(
        _router_kernel,
        grid=(rows // tm,),
        in_specs=[pl.BlockSpec((tm, d), lambda i: (i, 0)),
                  pl.BlockSpec((1, d), lambda i: (0, 0)),
                  pl.BlockSpec((ROUTER_ROWS, 3 * d), lambda i: (0, 0)),
                  pl.BlockSpec((ROUTER_ROWS, 1), lambda i: (0, 0))],
        out_specs=[pl.BlockSpec((2, tm), lambda i: (0, i)),
                   pl.BlockSpec((2, tm), lambda i: (0, i))],
        out_shape=[jax.ShapeDtypeStruct((2, rows), jnp.int32),
                   jax.ShapeDtypeStruct((2, rows), F32)],
        compiler_params=pltpu.CompilerParams(dimension_semantics=("arbitrary",)),
        name="router",
    )(x, g, w3, bias_col)


EXPERT_PHASES = 4


def _expert_kernel(te_ref, src_cur, src_next, dst_prev, dst_cur, roww_ref, g_ref,
                   wg_ref, wu_ref, wd_ref, x_hbm, y_hbm,
                   xbuf, ybuf, wgb, wub, wdb, gsem, ssem, *, tm):
    j = pl.program_id(0)
    last = pl.num_programs(0) - 1
    slot = j % 2
    other = 1 - slot

    def gather_rows(idx_ref, buf, lo, hi):
        for r in range(lo, hi):
            pltpu.make_async_copy(x_hbm.at[pl.ds(idx_ref[0, 0, r], 1)],
                                  xbuf.at[buf, pl.ds(r, 1)], gsem.at[buf]).start()

    def scatter_rows(idx_ref, buf, lo, hi):
        for r in range(lo, hi):
            pltpu.make_async_copy(ybuf.at[buf, pl.ds(r, 1)],
                                  y_hbm.at[pl.ds(idx_ref[0, 0, r], 1)], ssem.at[buf]).start()

    def wait_gather(buf):
        pltpu.make_async_copy(x_hbm.at[pl.ds(0, tm)], xbuf.at[buf], gsem.at[buf]).wait()

    def wait_scatter(buf):
        pltpu.make_async_copy(ybuf.at[buf], y_hbm.at[pl.ds(0, tm)], ssem.at[buf]).wait()

    @pl.when(j == 0)
    def _():
        ybuf[1] = jnp.zeros(ybuf.shape[1:], F32)
        gather_rows(src_cur, 0, 0, tm)

    @pl.when(jnp.logical_or(j == 0, te_ref[j] != te_ref[jnp.maximum(j - 1, 0)]))
    def _():
        wgb[...] = wg_ref[0].astype(BF16)
        wub[...] = wu_ref[0].astype(BF16)
        wdb[...] = wd_ref[0].astype(BF16)

    wait_gather(slot)

    @pl.when(j >= 1)
    def _():
        wait_scatter(slot)

    x = xbuf[slot]
    ms = jnp.mean(x * x, axis=-1, keepdims=True)
    xn = (x * lax.rsqrt(ms + RMS_EPS) * g_ref[...]).astype(BF16)
    roww = roww_ref[...]
    ff = wgb.shape[1]
    d = wdb.shape[1]
    step = tm // EXPERT_PHASES
    half_ff, half_d = ff // 2, d // 2
    hs = []
    for c in range(2):
        gather_rows(src_next, other, (2 * c) * step, (2 * c + 2) * step)
        cols = slice(c * half_ff, (c + 1) * half_ff)
        hg = _dot(xn, wgb[:, cols])
        hu = _dot(xn, wub[:, cols])
        hs.append((hg * jax.nn.sigmoid(hg) * hu * roww).astype(BF16))
    h = jnp.concatenate(hs, axis=1)
    for c in range(2):
        scatter_rows(dst_prev, other, (2 * c) * step, (2 * c + 2) * step)
        cols = slice(c * half_d, (c + 1) * half_d)
        ybuf[slot, :, cols] = _dot(h, wdb[:, cols])

    @pl.when(j == last)
    def _():
        scatter_rows(dst_cur, slot, 0, tm)
        wait_gather(other)
        wait_scatter(other)
        wait_scatter(slot)


def _experts(tile_expert, src_cur, src_next, dst_prev, dst_cur, row_w, g, w_gate, w_up, w_down,
             x, *, tm, out_rows):
    nt = tile_expert.shape[0]
    d = x.shape[1]
    ff = w_gate.shape[2]
    assert tm % EXPERT_PHASES == 0
    smem_row = pl.BlockSpec((1, 1, tm), lambda i, te: (i, 0, 0), memory_space=pltpu.SMEM)
    expert_w = lambda shape: pl.BlockSpec((1,) + shape, lambda i, te: (te[i], 0, 0))
    grid_spec = pltpu.PrefetchScalarGridSpec(
        num_scalar_prefetch=1,
        grid=(nt,),
        in_specs=[smem_row, smem_row, smem_row, smem_row,
                  pl.BlockSpec((tm, 1), lambda i, te: (i, 0)),
                  pl.BlockSpec((1, d), lambda i, te: (0, 0)),
                  expert_w((d, ff)), expert_w((d, ff)), expert_w((ff, d)),
                  pl.BlockSpec(memory_space=pl.ANY)],
        out_specs=pl.BlockSpec(memory_space=pl.ANY),
        scratch_shapes=[pltpu.VMEM((2, tm, d), F32), pltpu.VMEM((2, tm, d), F32),
                        pltpu.VMEM((d, ff), BF16), pltpu.VMEM((d, ff), BF16),
                        pltpu.VMEM((ff, d), BF16),
                        pltpu.SemaphoreType.DMA((2,)), pltpu.SemaphoreType.DMA((2,))])
    vmem = 2 * 3 * d * ff * 4 + 3 * d * ff * 2 + 4 * tm * d * 4 + 6 * tm * d * 4
    return pl.pallas_call(
        functools.partial(_expert_kernel, tm=tm),
        grid_spec=grid_spec,
        out_shape=jax.ShapeDtypeStruct((out_rows, d), F32),
        compiler_params=pltpu.CompilerParams(
            dimension_semantics=("arbitrary",), vmem_limit_bytes=_vmem_limit(vmem)),
        name="experts",
    )(tile_expert, src_cur, src_next, dst_prev, dst_cur, row_w, g, w_gate, w_up, w_down, x)


def _combine_kernel(x_ref, y0_ref, y1_ref, o_ref):
    o_ref[...] = x_ref[...] + (y0_ref[...] + y1_ref[...])


def _combine(x2, y2, row0, rows, total_rows):
    d = x2.shape[1]
    tm = math.gcd(math.gcd(rows, row0) if row0 else rows, LANES)
    assert total_rows % tm == 0 and tm % SUBLANES == 0
    b0 = row0 // tm
    b1 = (total_rows + row0) // tm
    return pl.pallas_call(
        _combine_kernel,
        grid=(rows // tm,),
        in_specs=[pl.BlockSpec((tm, d), lambda i: (b0 + i, 0)),
                  pl.BlockSpec((tm, d), lambda i: (b0 + i, 0)),
                  pl.BlockSpec((tm, d), lambda i: (b1 + i, 0))],
        out_specs=pl.BlockSpec((tm, d), lambda i: (i, 0)),
        out_shape=jax.ShapeDtypeStruct((rows, d), F32),
        compiler_params=pltpu.CompilerParams(dimension_semantics=("arbitrary",)),
        name="moe_combine",
    )(x2, y2, y2)


def _routing_plan(e_idx, w, tm):
    total = e_idx.shape[1]
    n_assign = 2 * total
    nt = -(-n_assign // tm) + N_EXPERTS
    e_flat = e_idx.reshape(-1)
    w_flat = w.reshape(-1)
    order = jnp.argsort(e_flat, stable=True).astype(jnp.int32)
    counts = jnp.sum((e_flat[:, None] == jnp.arange(N_EXPERTS, dtype=jnp.int32)[None, :])
                     .astype(jnp.int32), axis=0)
    tiles_per = (counts + tm - 1) // tm
    tile_end = jnp.cumsum(tiles_per)
    tile_start = tile_end - tiles_per
    group_start = jnp.cumsum(counts) - counts
    tiles = jnp.arange(nt, dtype=jnp.int32)
    tile_expert = jnp.minimum(
        jnp.sum((tiles[:, None] >= tile_end[None, :]).astype(jnp.int32), axis=1), N_EXPERTS - 1)
    tile_valid = (tiles < tile_end[-1]).astype(jnp.int32)
    rank = (tiles - tile_start[tile_expert])[:, None] * tm + jnp.arange(tm, dtype=jnp.int32)[None, :]
    valid = (rank < counts[tile_expert][:, None]) & (tile_valid[:, None] > 0)
    pos = jnp.clip(group_start[tile_expert][:, None] + rank, 0, n_assign - 1)
    assign = order[pos]
    pad_rank = jnp.cumsum(jnp.logical_not(valid).reshape(-1).astype(jnp.int32)) - 1
    dump = n_assign + pad_rank.reshape(nt, tm)
    row_src = jnp.where(valid, assign % total, 0)
    row_dst = jnp.where(valid, assign, dump)
    row_w = jnp.where(valid, w_flat[assign], 0.0)
    src_next = jnp.concatenate([row_src[1:], row_src[-1:]], axis=0)
    first_dst = nt * tm + jnp.arange(tm, dtype=jnp.int32)[None, :]
    dst_prev = jnp.concatenate([first_dst, row_dst[:-1]], axis=0)
    as_smem = lambda a: a.reshape(nt, 1, tm)
    return (tile_expert, as_smem(row_src), as_smem(src_next), as_smem(dst_prev), as_smem(row_dst),
            row_w.reshape(nt * tm, 1)), (nt + 1) * tm


def _group_matrices(aw):
    e = jnp.arange(aw, dtype=jnp.int32) // QK_HEAD_DIM
    lanes = jnp.arange(LANES, dtype=jnp.int32)
    member = (e[:, None] == lanes[None, :]).astype(BF16)
    return jnp.concatenate([member, member], axis=0), jnp.concatenate([member.T, member.T], axis=0)


def _toeplitz_kernel(strip_ref, out_ref, *, t):
    x = jnp.broadcast_to(strip_ref[0], (t, 2 * t))
    out_ref[0, 0] = pltpu.roll(x, t + 1, axis=1, stride=1, stride_axis=0)[:, :t]


def _prompt_bias(rel_table, t, nd):
    m = jnp.arange(2 * t, dtype=jnp.int32)
    dist = jnp.arange(nd + 1, dtype=jnp.int32)[:, None] * t - (t - 1) + m[None, :]
    strips = _bias_table(rel_table, dist.reshape(-1, LANES))
    strips = strips.reshape(N_HEADS * (nd + 1), 1, 2 * t)
    return pl.pallas_call(
        functools.partial(_toeplitz_kernel, t=t),
        grid=(N_HEADS, nd + 1),
        in_specs=[pl.BlockSpec((1, 1, 2 * t), lambda h, dl: (h * (nd + 1) + dl, 0, 0))],
        out_specs=pl.BlockSpec((1, 1, t, t), lambda h, dl: (h, dl, 0, 0)),
        out_shape=jax.ShapeDtypeStruct((N_HEADS, nd + 1, t, t), F32),
        name="bias_toeplitz",
    )(strips)


def _sample_bias(rel_table, past, n_pages, page, dt):
    tq = jnp.arange(dt, dtype=jnp.int32)
    kpos = jnp.arange(n_pages * page, dtype=jnp.int32).reshape(n_pages, 1, page)
    dist_pages = (past + tq[None, :, None] - kpos).reshape(-1, page)
    j = jnp.arange(page, dtype=jnp.int32)
    self_rows = jnp.where(j[None, :] < dt, tq[:, None] - j[None, :], -1)
    dist = jnp.concatenate([dist_pages, self_rows], axis=0)
    rows = dist.shape[0]
    rpad = (-rows) % (256 if rows > 256 else SUBLANES)
    dist = jnp.concatenate([dist, jnp.full((rpad, page), -1, jnp.int32)], axis=0)
    bias = _bias_table(rel_table, dist)
    bp = bias[:, :n_pages * dt].reshape(N_HEADS, n_pages, dt, page)
    bp = bp.transpose(1, 2, 0, 3).reshape(n_pages, dt * N_HEADS, page)
    bs = bias[:, n_pages * dt:n_pages * dt + dt].transpose(1, 0, 2).reshape(dt * N_HEADS, page)
    return bp, bs


def kernel(x_prompt, x_sample, cache_k, cache_v, state_conv, page_table, rel_table, attn_norm_g,
           w_in, q_norm_g, k_norm_g, lambda_q, lambda_k, sub_norm_g, conv_w, w_o, ffn_norm_g,
           router_group_w, router_group_b, router_expert_w, router_expert_b, w_gate, w_up, w_down):
    depth = w_in.shape[0]
    assert depth == 1, "single-layer trunk"
    batch, seq, d = x_prompt.shape
    assert batch == 1, "one prompt sequence"
    nb, dt, _ = x_sample.shape
    n_pool, page = cache_k.shape[1], cache_k.shape[2]
    n_pages = page_table.shape[1]
    past = n_pages * page
    aw = N_HEADS * V_HEAD_DIM
    layer = 0
    lam_init = 0.8 - 0.6 * math.exp(-0.3 * layer)

    w_in_b = w_in[layer].astype(BF16)
    w_o_b = w_o[layer].astype(BF16)
    g_attn = attn_norm_g[layer].reshape(1, d)
    g_ffn = ffn_norm_g[layer].reshape(1, d)
    qg = jnp.tile(q_norm_g[layer], aw // QK_HEAD_DIM).reshape(1, aw)
    kg = jnp.tile(k_norm_g[layer], aw // QK_HEAD_DIM).reshape(1, aw)
    g_sub = sub_norm_g[layer].reshape(1, V_HEAD_DIM)
    g_sub_t = jnp.tile(sub_norm_g[layer], N_HEADS).reshape(1, aw)
    gsum, gexp = _group_matrices(aw)
    lq, lk = lambda_q[layer], lambda_k[layer]
    cw = conv_w[layer]

    xp = x_prompt.reshape(seq, d)
    xs = x_sample.reshape(nb * dt, d)

    qtp, kp, kbp, vp, vtp, gbp, up = _project(xp, g_attn, w_in_b, qg, kg, gsum, gexp)
    qts, ks, _, vs, _, gbs, us = _project(xs, g_attn, w_in_b, qg, kg, gsum, gexp)

    t = min(seq, 512)
    assert seq % t == 0
    nd = min(-(-(MAX_DISTANCE + t - 1) // t), seq // t)
    bias_p = _prompt_bias(rel_table, t, nd)
    ap = _prompt_attention(qtp, kbp, vtp, bias_p, lq, lk, g_sub, t=t, nd=nd, lam_init=lam_init)

    bias_pages, bias_self = _sample_bias(rel_table, past, n_pages, page, dt)
    lane_hc = jnp.arange(aw, dtype=jnp.int32) // QK_HEAD_DIM
    row_c = jnp.arange(2, dtype=jnp.int32)[:, None, None]
    row_h = jnp.arange(N_HEADS, dtype=jnp.int32)[None, None, :]
    sel = (lane_hc[None, None, None, :] == (row_h * 2 + row_c)[..., None])
    q3 = qts.T.reshape(nb, 1, dt, 1, aw)
    q_rows = jnp.where(sel[None], q3, jnp.zeros((), BF16)).reshape(nb, 2 * dt * N_HEADS, aw)
    k3 = ks.reshape(nb, dt, aw).astype(BF16)
    kt_new = jnp.pad(k3.transpose(0, 2, 1), ((0, 0), (0, 0), (0, page - dt)))
    v_new = jnp.pad(vs.reshape(nb, dt, aw).astype(BF16), ((0, 0), (0, page - dt), (0, 0)))
    cache_kt = cache_k[layer].transpose(0, 2, 3, 4, 1).reshape(n_pool, aw, page)
    cache_vr = cache_v[layer].reshape(n_pool, page * N_HEADS, V_HEAD_DIM)
    pages_per_step = math.gcd(n_pages, 8)
    a_s = _sample_attention(page_table, q_rows, kt_new, v_new, bias_pages, bias_self,
                            lq, lk, g_sub_t, cache_kt, cache_vr,
                            pages=pages_per_step, lam_init=lam_init)
    a_s = a_s.reshape(nb * dt, aw)

    x2p = _out_project(ap, gbp, up, cw, w_o_b, xp)
    u_ext = jnp.concatenate([state_conv[layer].astype(F32), us.reshape(nb, dt, aw)], axis=1)
    prev1 = u_ext[:, 1:1 + dt].reshape(nb * dt, aw)
    prev2 = u_ext[:, 0:dt].reshape(nb * dt, aw)
    x2s = _out_project(a_s, gbs, us, cw, w_o_b, xs, prev=(prev1, prev2))
    x2 = jnp.concatenate([x2p, x2s], axis=0)
    total = seq + nb * dt

    wr = jnp.zeros((ROUTER_ROWS, d), F32)
    wr = wr.at[0:N_GROUPS].set(router_group_w[layer].T)
    wr = wr.at[EXPERT_ROW0:EXPERT_ROW0 + N_EXPERTS].set(router_expert_w[layer].T)
    wr_hi, wr_lo = _split_bf16(wr)
    w3 = jnp.concatenate([wr_hi, wr_lo, wr_hi], axis=1)
    bias_col = jnp.full((ROUTER_ROWS,), NEG_INF, F32)
    bias_col = bias_col.at[0:N_GROUPS].set(router_group_b[layer].astype(F32))
    bias_col = bias_col.at[EXPERT_ROW0:EXPERT_ROW0 + N_EXPERTS].set(
        router_expert_b[layer].reshape(-1).astype(F32)).reshape(ROUTER_ROWS, 1)
    e_idx, wts = _route(x2, g_ffn, w3, bias_col)
    tm_e = 256
    plan, y_rows = _routing_plan(e_idx, wts, tm_e)
    y2 = _experts(*plan, g_ffn, w_gate[layer], w_up[layer], w_down[layer], x2,
                  tm=tm_e, out_rows=y_rows)
    yp = _combine(x2, y2, 0, seq, total)
    ys = _combine(x2, y2, seq, nb * dt, total)

    return (yp.reshape(batch, seq, d),
            ys.reshape(nb, dt, d),
            kp.reshape(depth, batch, seq, N_HEADS, 2, QK_HEAD_DIM),
            vp.reshape(depth, batch, seq, N_HEADS, V_HEAD_DIM),
            up[seq - (CONV_K - 1):].reshape(depth, batch, CONV_K - 1, aw),
            ks.reshape(depth, nb, dt, N_HEADS, 2, QK_HEAD_DIM),
            vs.reshape(depth, nb, dt, N_HEADS, V_HEAD_DIM),
            u_ext[:, dt:].reshape(depth, nb, CONV_K - 1, aw))
```

```python
import functools
import math

import jax
import jax.numpy as jnp
from jax import lax
from jax.experimental import pallas as pl
from jax.experimental.pallas import tpu as pltpu

N_HEADS = 8
V_HEAD_DIM = 128
QK_HEAD_DIM = 64
CONV_K = 3
N_BUCKETS = 32
MAX_EXACT = N_BUCKETS // 2
MAX_DISTANCE = 2048
N_GROUPS = 4
EXPERTS_PER_GROUP = 4
N_EXPERTS = N_GROUPS * EXPERTS_PER_GROUP
RMS_EPS = 1e-6
NEG_INF = -1e30
LOG2E = math.log2(math.e)

LANES = 128
SUBLANES = 8
VMEM_PHYSICAL_BYTES = 64 * 1024 * 1024

F32 = jnp.float32
BF16 = jnp.bfloat16


def _vmem_limit(nbytes):
    return int(min(max(nbytes * 5 // 4 + (4 << 20), 16 << 20), VMEM_PHYSICAL_BYTES - (6 << 20)))


def _dot(a, b):
    return jnp.dot(a, b, preferred_element_type=F32)


def _dot_nt(a, b):
    return lax.dot_general(a, b, (((1,), (1,)), ((), ())), preferred_element_type=F32)


def _split_bf16(x):
    hi = x.astype(BF16)
    lo = (x - hi.astype(F32)).astype(BF16)
    return hi, lo


def _load_token_major(ref, rows, chunks, lead=()):
    return jnp.concatenate(
        [ref[lead + (pl.ds(k, rows, stride=chunks), slice(None))] for k in range(chunks)], axis=1)


def _store_token_major(ref, x, lead=()):
    rows, d = x.shape
    chunks = d // LANES
    for k in range(chunks):
        ref[lead + (pl.ds(k, rows, stride=chunks), slice(None))] = x[:, k * LANES:(k + 1) * LANES]


def _bias_kernel(table_ref, dist_ref, out_ref):
    d = dist_ref[...]
    n = jnp.maximum(d, 0)
    nf = jnp.maximum(n, 1).astype(F32)
    large = MAX_EXACT + (jnp.log(nf / MAX_EXACT) / math.log(MAX_DISTANCE / MAX_EXACT)
                         * (N_BUCKETS - MAX_EXACT)).astype(jnp.int32)
    large = jnp.minimum(large, N_BUCKETS - 1)
    bucket = jnp.where(n < MAX_EXACT, n, large)
    hits = [bucket == b for b in range(N_BUCKETS)]
    for h in range(N_HEADS):
        acc = jnp.zeros(d.shape, F32)
        for b in range(N_BUCKETS):
            acc = jnp.where(hits[b], table_ref[b, h] * LOG2E, acc)
        out_ref[h] = jnp.where(d < 0, NEG_INF, acc)


def _bias_table(rel_table, dist):
    rows = dist.shape[0]
    tr = min(rows, 256)
    assert rows % tr == 0
    return pl.pallas_call(
        _bias_kernel,
        grid=(rows // tr,),
        in_specs=[pl.BlockSpec(memory_space=pltpu.SMEM),
                  pl.BlockSpec((tr, LANES), lambda i: (i, 0))],
        out_specs=pl.BlockSpec((N_HEADS, tr, LANES), lambda i: (0, i, 0)),
        out_shape=jax.ShapeDtypeStruct((N_HEADS, rows, LANES), F32),
        name="rel_bias",
    )(rel_table.astype(F32), dist)


def _group_rms_scale(z, gsum, gexp):
    hi, lo = _split_bf16(z * z)
    ssq = _dot(jnp.concatenate([hi, lo], axis=1), gsum)
    r = lax.rsqrt(ssq * (1.0 / QK_HEAD_DIM) + RMS_EPS)
    r_hi, r_lo = _split_bf16(r)
    return _dot(jnp.concatenate([r_hi, r_lo], axis=1), gexp)


def _proj_kernel(x_ref, g_ref, w_ref, qg_ref, kg_ref, gsum_ref, gexp_ref,
                 qt_ref, kf_ref, kb_ref, vf_ref, vt_ref, gb_ref, u_ref, *, aw):
    x = x_ref[...]
    ms = jnp.mean(x * x, axis=-1, keepdims=True)
    xn = (x * lax.rsqrt(ms + RMS_EPS) * g_ref[...]).astype(BF16)
    gsum = gsum_ref[...]
    gexp = gexp_ref[...]

    zq = _dot(xn, w_ref[:, 0 * aw:1 * aw])
    q = zq * _group_rms_scale(zq, gsum, gexp) * qg_ref[...]
    qt_ref[...] = (q * (QK_HEAD_DIM ** -0.5 * LOG2E)).T.astype(BF16)

    zk = _dot(xn, w_ref[:, 1 * aw:2 * aw])
    k = zk * _group_rms_scale(zk, gsum, gexp) * kg_ref[...]
    kf_ref[...] = k
    kb_ref[...] = k.astype(BF16)

    v = _dot(xn, w_ref[:, 2 * aw:3 * aw])
    vf_ref[...] = v
    vt_ref[...] = v.T.astype(BF16)

    gb_ref[...] = _dot(xn, w_ref[:, 3 * aw:4 * aw])
    u_ref[...] = _dot(xn, w_ref[:, 4 * aw:5 * aw]) * _dot(xn, w_ref[:, 5 * aw:6 * aw])


def _project(x, g, w_bf16, q_gain, k_gain, gsum, gexp):
    rows, d = x.shape
    pw = w_bf16.shape[1]
    aw = pw // 6
    tm = min(rows, 256)
    assert rows % tm == 0 and aw % LANES == 0
    row_blk = lambda width: pl.BlockSpec((tm, width), lambda i: (i, 0))
    col_blk = pl.BlockSpec((aw, tm), lambda i: (0, i))
    const = lambda shape: pl.BlockSpec(shape, lambda i: (0, 0))
    vmem = (d * pw * 2 + 2 * tm * d * 4 + 2 * tm * aw * (2 + 4 + 2 + 4 + 2 + 4 + 4)
            + 8 * tm * aw * 4 + 4 * aw * (2 * LANES + 2 * LANES))
    return pl.pallas_call(
        functools.partial(_proj_kernel, aw=aw),
        grid=(rows // tm,),
        in_specs=[row_blk(d), const((1, d)),
                  pl.BlockSpec((d, pw), lambda i: (0, 0), pipeline_mode=pl.Buffered(1)),
                  const((1, aw)), const((1, aw)),
                  const((2 * aw, LANES)), const((2 * LANES, aw))],
        out_specs=[col_blk, row_blk(aw), row_blk(aw), row_blk(aw), col_blk,
                   row_blk(aw), row_blk(aw)],
        out_shape=[jax.ShapeDtypeStruct((aw, rows), BF16),
                   jax.ShapeDtypeStruct((rows, aw), F32),
                   jax.ShapeDtypeStruct((rows, aw), BF16),
                   jax.ShapeDtypeStruct((rows, aw), F32),
                   jax.ShapeDtypeStruct((aw, rows), BF16),
                   jax.ShapeDtypeStruct((rows, aw), F32),
                   jax.ShapeDtypeStruct((rows, aw), F32)],
        compiler_params=pltpu.CompilerParams(
            dimension_semantics=("arbitrary",), vmem_limit_bytes=_vmem_limit(vmem)),
        name="in_proj",
    )(x, g, w_bf16, q_gain, k_gain, gsum, gexp)


def _lambda(lq_ref, lk_ref, lam_init):
    e = jnp.exp(jnp.sum(lq_ref[...] * lk_ref[...], axis=-1, keepdims=True))
    return e[0:1] - e[1:2] + lam_init


def _prompt_attn_kernel(qt_ref, k_ref, vt_ref, bias_ref, lq_ref, lk_ref, gs_ref,
                        o_ref, m_scr, l_scr, acc_scr, sa_scr, sb_scr, *, t, nd, lam_init):
    qi = pl.program_id(1)
    qt = qt_ref[...]
    row = lax.broadcasted_iota(jnp.int32, qt.shape, 0)
    zero = jnp.zeros_like(qt)
    qmaps = (jnp.where(row < QK_HEAD_DIM, qt, zero), jnp.where(row >= QK_HEAD_DIM, qt, zero))
    m_scr[...] = jnp.full(m_scr.shape, NEG_INF, F32)
    l_scr[...] = jnp.zeros(l_scr.shape, F32)
    acc_scr[...] = jnp.zeros(acc_scr.shape, F32)

    def scores(ki, s_scr):
        kb = k_ref[pl.ds(pl.multiple_of(ki * t, t), t), :]
        for c in range(2):
            s_scr[c] = _dot(kb, qmaps[c])

    def attend(ki, s_scr, delta):
        vt = vt_ref[:, pl.ds(pl.multiple_of(ki * t, t), t)]
        b = bias_ref[0, delta]
        for c in range(2):
            s = s_scr[c] + b
            m_prev = m_scr[c]
            m_new = jnp.maximum(m_prev, jnp.max(s, axis=0, keepdims=True))
            p = jnp.exp2(s - m_new)
            alpha = jnp.exp2(m_prev - m_new)
            l_scr[c] = alpha * l_scr[c] + jnp.sum(p, axis=0, keepdims=True)
            acc_scr[c] = alpha * acc_scr[c] + _dot(vt, p.astype(BF16))
            m_scr[c] = m_new

    scores(0, sa_scr)

    def body(j, carry):
        k0 = 2 * j
        scores(k0 + 1, sb_scr)
        attend(k0, sa_scr, jnp.minimum(qi - k0, nd))
        scores(k0 + 2, sa_scr)
        attend(k0 + 1, sb_scr, jnp.minimum(qi - k0 - 1, nd))
        return carry

    lax.fori_loop(0, qi // 2, body, 0)

    @pl.when(qi % 2 == 0)
    def _():
        attend(qi, sa_scr, 0)

    @pl.when(qi % 2 == 1)
    def _():
        scores(qi, sb_scr)
        attend(qi - 1, sa_scr, min(1, nd))
        attend(qi, sb_scr, 0)

    lam = _lambda(lq_ref, lk_ref, lam_init)
    o = acc_scr[0] / l_scr[0] - lam * (acc_scr[1] / l_scr[1])
    o = o * lax.rsqrt(jnp.mean(o * o, axis=0, keepdims=True) + RMS_EPS)
    o_ref[...] = (o.T * gs_ref[...] * (1.0 - lam_init)).astype(o_ref.dtype)


def _prompt_attention(qt, kb, vt, bias, lq, lk, g_sub, *, t, nd, lam_init):
    aw, s = qt.shape
    vmem = (2 * t * LANES * 2 + 2 * LANES * s * 2 + 2 * s * LANES * 2
            + 2 * (nd + 1) * t * t * 4 + 2 * t * LANES * 2
            + 2 * t * (2 * SUBLANES + LANES) * 4 + 4 * t * t * 4 + 8 * t * t * 4)
    return pl.pallas_call(
        functools.partial(_prompt_attn_kernel, t=t, nd=nd, lam_init=lam_init),
        grid=(N_HEADS, s // t),
        in_specs=[pl.BlockSpec((V_HEAD_DIM, t), lambda h, i: (h, i)),
                  pl.BlockSpec((s, V_HEAD_DIM), lambda h, i: (0, h)),
                  pl.BlockSpec((V_HEAD_DIM, s), lambda h, i: (h, 0)),
                  pl.BlockSpec((1, nd + 1, t, t), lambda h, i: (h, 0, 0, 0)),
                  pl.BlockSpec((2, QK_HEAD_DIM), lambda h, i: (0, 0)),
                  pl.BlockSpec((2, QK_HEAD_DIM), lambda h, i: (0, 0)),
                  pl.BlockSpec((1, V_HEAD_DIM), lambda h, i: (0, 0))],
        out_specs=pl.BlockSpec((t, V_HEAD_DIM), lambda h, i: (i, h)),
        out_shape=jax.ShapeDtypeStruct((s, aw), BF16),
        scratch_shapes=[pltpu.VMEM((2, 1, t), F32), pltpu.VMEM((2, 1, t), F32),
                        pltpu.VMEM((2, V_HEAD_DIM, t), F32),
                        pltpu.VMEM((2, t, t), F32), pltpu.VMEM((2, t, t), F32)],
        compiler_params=pltpu.CompilerParams(
            dimension_semantics=("arbitrary", "arbitrary"),
            vmem_limit_bytes=_vmem_limit(vmem)),
        name="prompt_attn",
    )(qt, kb, vt, bias, lq, lk, g_sub)


def _sample_attn_kernel(pt_ref, qr_ref, kn_ref, vn_ref, bias_ref, bself_ref, lq_ref, lk_ref,
                        gs_ref, *rest, pages, page, lam_init):
    del pt_ref
    k_refs = rest[:pages]
    v_refs = rest[pages:2 * pages]
    o_ref, m_scr, l_scr, acc_scr = rest[2 * pages:]
    step = pl.program_id(1)
    qr = qr_ref[0]
    half = qr.shape[0] // 2

    def both_maps(b):
        return jnp.concatenate([b, b], axis=0)

    def update(s, vs):
        m_prev = m_scr[...]
        m_new = jnp.maximum(m_prev, jnp.max(s, axis=-1, keepdims=True))
        p = jnp.exp2(s - m_new)
        alpha = jnp.exp2(m_prev - m_new)
        l_scr[...] = alpha * l_scr[...] + jnp.sum(p, axis=-1, keepdims=True)
        p = p.astype(BF16)
        acc = alpha * acc_scr[...]
        for j, vj in enumerate(vs):
            acc = acc + _dot(p[:, j * page:(j + 1) * page], vj)
        acc_scr[...] = acc
        m_scr[...] = m_new

    @pl.when(step == 0)
    def _():
        m_scr[...] = jnp.full(m_scr.shape, NEG_INF, F32)
        l_scr[...] = jnp.zeros(l_scr.shape, F32)
        acc_scr[...] = jnp.zeros(acc_scr.shape, F32)
        update(_dot(qr, kn_ref[0]) + both_maps(bself_ref[...]), [vn_ref[0]])

    def v_page(ref):
        return jnp.concatenate(
            [ref[0, pl.ds(h, page, stride=N_HEADS), :] for h in range(N_HEADS)],
            axis=1).astype(BF16)

    s = jnp.concatenate(
        [_dot(qr, k_refs[j][0].astype(BF16)) + both_maps(bias_ref[j]) for j in range(pages)],
        axis=1)
    update(s, [v_page(v_refs[j]) for j in range(pages)])

    @pl.when(step == pl.num_programs(1) - 1)
    def _():
        o = acc_scr[...] / l_scr[...]
        lam = _lambda(lq_ref, lk_ref, lam_init)
        o = o[:half] - lam * o[half:]
        row_head = lax.broadcasted_iota(jnp.int32, o.shape, 0) % N_HEADS
        lane_head = lax.broadcasted_iota(jnp.int32, o.shape, 1) // V_HEAD_DIM
        o = jnp.where(row_head == lane_head, o, 0.0)
        ms = jnp.sum(o * o, axis=-1, keepdims=True) * (1.0 / V_HEAD_DIM)
        o = o * lax.rsqrt(ms + RMS_EPS)
        o = jnp.sum(o.reshape(half // N_HEADS, N_HEADS, o.shape[1]), axis=1)
        o_ref[0] = (o * gs_ref[...] * (1.0 - lam_init)).astype(o_ref.dtype)


def _sample_attention(page_table, q_rows, kt_new, v_new, bias_pages, bias_self, lq, lk, g_sub_t,
                      cache_kt, cache_vr, *, pages, lam_init):
    nb, n_pages = page_table.shape
    _, aw, page = cache_kt.shape
    dt = q_rows.shape[1] // (2 * N_HEADS)
    steps = n_pages // pages
    assert n_pages % pages == 0
    rows = q_rows.shape[1]

    def page_spec(shape):
        return [pl.BlockSpec((1,) + shape,
                             lambda b, s, pt, j=j: (pt[b * n_pages + s * pages + j], 0, 0))
                for j in range(pages)]

    per_b = lambda shape: pl.BlockSpec(shape, lambda b, s, pt: (b, 0, 0))
    const2 = lambda shape: pl.BlockSpec(shape, lambda b, s, pt: (0, 0))
    vmem = (2 * 2 * pages * page * aw * 4 + 3 * pages * page * aw * 2
            + 4 * rows * aw * 4 + 2 * 3 * rows * aw * 2)
    grid_spec = pltpu.PrefetchScalarGridSpec(
        num_scalar_prefetch=1,
        grid=(nb, steps),
        in_specs=[per_b((1, rows, aw)), per_b((1, aw, page)), per_b((1, page, aw)),
                  pl.BlockSpec((pages, rows // 2, page), lambda b, s, pt: (s, 0, 0)),
                  const2((rows // 2, page)),
                  const2((2, QK_HEAD_DIM)), const2((2, QK_HEAD_DIM)), const2((1, aw))]
                 + page_spec((aw, page)) + page_spec((page * N_HEADS, V_HEAD_DIM)),
        out_specs=pl.BlockSpec((1, dt, aw), lambda b, s, pt: (b, 0, 0)),
        scratch_shapes=[pltpu.VMEM((rows, 1), F32), pltpu.VMEM((rows, 1), F32),
                        pltpu.VMEM((rows, aw), F32)])
    return pl.pallas_call(
        functools.partial(_sample_attn_kernel, pages=pages, page=page, lam_init=lam_init),
        grid_spec=grid_spec,
        out_shape=jax.ShapeDtypeStruct((nb, dt, aw), BF16),
        compiler_params=pltpu.CompilerParams(
            dimension_semantics=("arbitrary", "arbitrary"),
            vmem_limit_bytes=_vmem_limit(vmem)),
        name="sample_attn",
    )(page_table.reshape(-1), q_rows, kt_new, v_new, bias_pages, bias_self, lq, lk, g_sub_t,
      *([cache_kt] * pages), *([cache_vr] * pages))


ROUTER_ROWS = 32
EXPERT_ROW0 = 8


def _route_tile(x, g, w3, bias_col):
    ms = jnp.mean(x * x, axis=-1, keepdims=True)
    xn = x * lax.rsqrt(ms + RMS_EPS) * g
    hi, lo = _split_bf16(xn)
    logits = _dot_nt(w3, jnp.concatenate([hi, hi, lo], axis=1)) + bias_col
    big = jnp.int32(1 << 20)

    gl = logits[0:SUBLANES]
    grow = lax.broadcasted_iota(jnp.int32, gl.shape, 0)
    gmax = jnp.max(gl, axis=0, keepdims=True)
    g_idx = jnp.min(jnp.where(gl == gmax, grow, big), axis=0, keepdims=True)
    g_w = 1.0 / jnp.sum(jnp.exp(gl - gmax), axis=0, keepdims=True)

    el = logits[EXPERT_ROW0:EXPERT_ROW0 + N_EXPERTS]
    erow = lax.broadcasted_iota(jnp.int32, el.shape, 0)
    el = jnp.where(erow // EXPERTS_PER_GROUP == g_idx, el, NEG_INF)
    v1 = jnp.max(el, axis=0, keepdims=True)
    i1 = jnp.min(jnp.where(el == v1, erow, big), axis=0, keepdims=True)
    el2 = jnp.where(erow == i1, NEG_INF, el)
    v2 = jnp.max(el2, axis=0, keepdims=True)
    i2 = jnp.min(jnp.where(el2 == v2, erow, big), axis=0, keepdims=True)
    t = jnp.exp(v2 - v1)
    w1 = g_w / (1.0 + t)
    return jnp.concatenate([i1, i2], axis=0), jnp.concatenate([w1, w1 * t], axis=0)


def _out_proj_kernel(*refs, halo):
    if halo:
        (a_ref, gb_ref, u_ref, prev_ref, cw_ref, wo_ref, x_ref, g_ref, w3_ref, rb_ref,
         o_ref, idx_ref, wt_ref) = refs
        u = u_ref[...]
        first = pl.program_id(0) == 0
        prev = jnp.where(first, 0.0, prev_ref[...])
        ucat = jnp.concatenate([prev, u], axis=0)
        p1 = pltpu.roll(ucat, 1, axis=0)[SUBLANES:]
        p2 = pltpu.roll(ucat, 2, axis=0)[SUBLANES:]
    else:
        (a_ref, gb_ref, u_ref, p1_ref, p2_ref, cw_ref, wo_ref, x_ref, g_ref, w3_ref, rb_ref,
         o_ref, idx_ref, wt_ref) = refs
        u, p1, p2 = u_ref[...], p1_ref[...], p2_ref[...]
    cw = cw_ref[...]
    aw = u.shape[1]
    c = gb_ref[...] * (p2 * cw[0:1] + p1 * cw[1:2] + u * cw[2:3])
    y = _dot(a_ref[...], wo_ref[:aw, :]) + _dot(c.astype(BF16), wo_ref[aw:, :])
    x2 = x_ref[...] + y
    _store_token_major(o_ref, x2)
    idx_ref[...], wt_ref[...] = _route_tile(x2, g_ref[...], w3_ref[...], rb_ref[...])


def _out_project(a, gb, u, conv_w, wo_bf16, x, g_ffn, w3, bias_col, prev=None):
    rows, d = x.shape
    aw = a.shape[1]
    tm = min(rows, 256)
    assert rows % tm == 0 and tm % LANES == 0
    row_blk = lambda width: pl.BlockSpec((tm, width), lambda i: (i, 0))
    const = lambda shape: pl.BlockSpec(shape, lambda i: (0, 0))
    halo = prev is None
    if halo:
        nsub = tm // SUBLANES
        extra = [pl.BlockSpec((SUBLANES, aw), lambda i: (jnp.maximum(i * nsub - 1, 0), 0))]
        extra_args = [u]
    else:
        extra = [row_blk(aw), row_blk(aw)]
        extra_args = list(prev)
    vmem = (2 * d * d * 2 + 2 * tm * (aw * (2 + 4 + 4 + 8) + d * 8) + 10 * tm * d * 4
            + 2 * ROUTER_ROWS * 3 * d * 2)
    return pl.pallas_call(
        functools.partial(_out_proj_kernel, halo=halo),
        grid=(rows // tm,),
        in_specs=[row_blk(aw), row_blk(aw), row_blk(aw)] + extra
                 + [const((CONV_K, aw)), const((d, d)), row_blk(d),
                    const((1, d)), const((ROUTER_ROWS, 3 * d)), const((ROUTER_ROWS, 1))],
        out_specs=[pl.BlockSpec((tm * (d // LANES), LANES), lambda i: (i, 0)),
                   pl.BlockSpec((2, tm), lambda i: (0, i)),
                   pl.BlockSpec((2, tm), lambda i: (0, i))],
        out_shape=[jax.ShapeDtypeStruct((rows * (d // LANES), LANES), F32),
                   jax.ShapeDtypeStruct((2, rows), jnp.int32),
                   jax.ShapeDtypeStruct((2, rows), F32)],
        compiler_params=pltpu.CompilerParams(
            dimension_semantics=("arbitrary",), vmem_limit_bytes=_vmem_limit(vmem)),
        name="out_proj",
    )(a, gb, u, *extra_args, conv_w, wo_bf16, x, g_ffn, w3, bias_col)


EXPERT_PHASES = 4


def _expert_kernel(te_ref, src_cur, src_next, dst_prev, dst_cur, roww_ref, g_ref,
                   wg_ref, wu_ref, wd_ref, x_hbm, y_hbm,
                   xbuf, ybuf, wgb, wub, wdb, gsem, ssem, *, tm):
    j = pl.program_id(0)
    last = pl.num_programs(0) - 1
    slot = j % 2
    other = 1 - slot

    chunks = g_ref.shape[1] // LANES

    def token(ref, start):
        return ref.at[pl.ds(pl.multiple_of(start, chunks), chunks)]

    def gather_rows(idx_ref, buf, lo, hi):
        for r in range(lo, hi):
            pltpu.make_async_copy(token(x_hbm, idx_ref[0, 0, r]),
                                  xbuf.at[buf, pl.ds(r * chunks, chunks)], gsem.at[buf]).start()

    def scatter_rows(idx_ref, buf, lo, hi):
        for r in range(lo, hi):
            pltpu.make_async_copy(ybuf.at[buf, pl.ds(r * chunks, chunks)],
                                  token(y_hbm, idx_ref[0, 0, r]), ssem.at[buf]).start()

    def wait_gather(buf):
        pltpu.make_async_copy(x_hbm.at[pl.ds(0, tm * chunks)], xbuf.at[buf], gsem.at[buf]).wait()

    def wait_scatter(buf):
        pltpu.make_async_copy(ybuf.at[buf], y_hbm.at[pl.ds(0, tm * chunks)], ssem.at[buf]).wait()

    @pl.when(j == 0)
    def _():
        ybuf[1] = jnp.zeros(ybuf.shape[1:], F32)
        gather_rows(src_cur, 0, 0, tm)

    @pl.when(jnp.logical_or(j == 0, te_ref[j] != te_ref[jnp.maximum(j - 1, 0)]))
    def _():
        wgb[...] = wg_ref[0].astype(BF16)
        wub[...] = wu_ref[0].astype(BF16)
        wdb[...] = wd_ref[0].astype(BF16)

    wait_gather(slot)

    @pl.when(j >= 1)
    def _():
        wait_scatter(slot)

    x = _load_token_major(xbuf, tm, chunks, lead=(slot,))
    ms = jnp.mean(x * x, axis=-1, keepdims=True)
    xn = (x * lax.rsqrt(ms + RMS_EPS) * g_ref[...]).astype(BF16)
    roww = roww_ref[...]
    ff = wgb.shape[1]
    d = wdb.shape[1]
    step = tm // EXPERT_PHASES
    half_ff, half_d = ff // 2, d // 2
    hs = []
    for c in range(2):
        gather_rows(src_next, other, (2 * c) * step, (2 * c + 2) * step)
        cols = slice(c * half_ff, (c + 1) * half_ff)
        hg = _dot(xn, wgb[:, cols])
        hu = _dot(xn, wub[:, cols])
        hs.append((hg * jax.nn.sigmoid(hg) * hu * roww).astype(BF16))
    h = jnp.concatenate(hs, axis=1)
    for c in range(2):
        scatter_rows(dst_prev, other, (2 * c) * step, (2 * c + 2) * step)
        y = _dot(h, wdb[:, c * half_d:(c + 1) * half_d])
        for k in range(half_d // LANES):
            ybuf[slot, pl.ds(c * (half_d // LANES) + k, tm, stride=chunks), :] = (
                y[:, k * LANES:(k + 1) * LANES])

    @pl.when(j == last)
    def _():
        scatter_rows(dst_cur, slot, 0, tm)
        wait_gather(other)
        wait_scatter(other)
        wait_scatter(slot)


def _experts(tile_expert, src_cur, src_next, dst_prev, dst_cur, row_w, g, w_gate, w_up, w_down,
             x, *, tm, out_rows):
    nt = tile_expert.shape[0]
    d = g.shape[1]
    chunks = d // LANES
    ff = w_gate.shape[2]
    assert tm % EXPERT_PHASES == 0
    smem_row = pl.BlockSpec((1, 1, tm), lambda i, te: (i, 0, 0), memory_space=pltpu.SMEM)
    expert_w = lambda shape: pl.BlockSpec((1,) + shape, lambda i, te: (te[i], 0, 0))
    grid_spec = pltpu.PrefetchScalarGridSpec(
        num_scalar_prefetch=1,
        grid=(nt,),
        in_specs=[smem_row, smem_row, smem_row, smem_row,
                  pl.BlockSpec((tm, 1), lambda i, te: (i, 0)),
                  pl.BlockSpec((1, d), lambda i, te: (0, 0)),
                  expert_w((d, ff)), expert_w((d, ff)), expert_w((ff, d)),
                  pl.BlockSpec(memory_space=pl.ANY)],
        out_specs=pl.BlockSpec(memory_space=pl.ANY),
        scratch_shapes=[pltpu.VMEM((2, tm * chunks, LANES), F32),
                        pltpu.VMEM((2, tm * chunks, LANES), F32),
                        pltpu.VMEM((d, ff), BF16), pltpu.VMEM((d, ff), BF16),
                        pltpu.VMEM((ff, d), BF16),
                        pltpu.SemaphoreType.DMA((2,)), pltpu.SemaphoreType.DMA((2,))])
    vmem = 2 * 3 * d * ff * 4 + 3 * d * ff * 2 + 4 * tm * d * 4 + 6 * tm * d * 4
    return pl.pallas_call(
        functools.partial(_expert_kernel, tm=tm),
        grid_spec=grid_spec,
        out_shape=jax.ShapeDtypeStruct((out_rows * chunks, LANES), F32),
        compiler_params=pltpu.CompilerParams(
            dimension_semantics=("arbitrary",), vmem_limit_bytes=_vmem_limit(vmem)),
        name="experts",
    )(tile_expert, src_cur, src_next, dst_prev, dst_cur, row_w, g, w_gate, w_up, w_down, x)


def _combine_kernel(x_ref, y0_ref, y1_ref, o_ref):
    rows, d = o_ref.shape
    chunks = d // LANES
    for k in range(chunks):
        rows_k = pl.ds(k, rows, stride=chunks)
        o_ref[:, k * LANES:(k + 1) * LANES] = (
            x_ref[rows_k, :] + (y0_ref[rows_k, :] + y1_ref[rows_k, :]))


def _combine(x2, y2, row0, rows, total_rows, d):
    chunks = d // LANES
    tm = math.gcd(math.gcd(rows, row0) if row0 else rows, LANES)
    assert total_rows % tm == 0 and tm % SUBLANES == 0
    b0 = row0 // tm
    b1 = (total_rows + row0) // tm
    tm_blk = lambda first: pl.BlockSpec((tm * chunks, LANES), lambda i: (first + i, 0))
    return pl.pallas_call(
        _combine_kernel,
        grid=(rows // tm,),
        in_specs=[tm_blk(b0), tm_blk(b0), tm_blk(b1)],
        out_specs=pl.BlockSpec((tm, d), lambda i: (i, 0)),
        out_shape=jax.ShapeDtypeStruct((rows, d), F32),
        compiler_params=pltpu.CompilerParams(dimension_semantics=("arbitrary",)),
        name="moe_combine",
    )(x2, y2, y2)


def _routing_plan(e_idx, w, tm, chunks):
    total = e_idx.shape[1]
    n_assign = 2 * total
    nt = -(-n_assign // tm) + N_EXPERTS
    e_flat = e_idx.reshape(-1)
    w_flat = w.reshape(-1)
    order = jnp.argsort(e_flat, stable=True).astype(jnp.int32)
    counts = jnp.sum((e_flat[:, None] == jnp.arange(N_EXPERTS, dtype=jnp.int32)[None, :])
                     .astype(jnp.int32), axis=0)
    tiles_per = (counts + tm - 1) // tm
    tile_end = jnp.cumsum(tiles_per)
    tile_start = tile_end - tiles_per
    group_start = jnp.cumsum(counts) - counts
    tiles = jnp.arange(nt, dtype=jnp.int32)
    tile_expert = jnp.minimum(
        jnp.sum((tiles[:, None] >= tile_end[None, :]).astype(jnp.int32), axis=1), N_EXPERTS - 1)
    tile_valid = (tiles < tile_end[-1]).astype(jnp.int32)
    rank = (tiles - tile_start[tile_expert])[:, None] * tm + jnp.arange(tm, dtype=jnp.int32)[None, :]
    valid = (rank < counts[tile_expert][:, None]) & (tile_valid[:, None] > 0)
    pos = jnp.clip(group_start[tile_expert][:, None] + rank, 0, n_assign - 1)
    assign = order[pos]
    pad_rank = jnp.cumsum(jnp.logical_not(valid).reshape(-1).astype(jnp.int32)) - 1
    dump = n_assign + pad_rank.reshape(nt, tm)
    row_src = jnp.where(valid, assign % total, 0)
    row_dst = jnp.where(valid, assign, dump)
    row_w = jnp.where(valid, w_flat[assign], 0.0)
    src_next = jnp.concatenate([row_src[1:], row_src[-1:]], axis=0)
    first_dst = nt * tm + jnp.arange(tm, dtype=jnp.int32)[None, :]
    dst_prev = jnp.concatenate([first_dst, row_dst[:-1]], axis=0)
    as_smem = lambda a: (a * chunks).reshape(nt, 1, tm)
    return (tile_expert, as_smem(row_src), as_smem(src_next), as_smem(dst_prev), as_smem(row_dst),
            row_w.reshape(nt * tm, 1)), (nt + 1) * tm


def _group_matrices(aw):
    e = jnp.arange(aw, dtype=jnp.int32) // QK_HEAD_DIM
    lanes = jnp.arange(LANES, dtype=jnp.int32)
    member = (e[:, None] == lanes[None, :]).astype(BF16)
    return jnp.concatenate([member, member], axis=0), jnp.concatenate([member.T, member.T], axis=0)


def _toeplitz_kernel(strip_ref, out_ref, *, t):
    x = jnp.broadcast_to(strip_ref[0], (t, 2 * t))
    out_ref[0, 0] = pltpu.roll(x, t + 1, axis=1, stride=1, stride_axis=0)[:, :t]


def _prompt_bias(rel_table, t, nd):
    m = jnp.arange(2 * t, dtype=jnp.int32)
    dist = jnp.arange(nd + 1, dtype=jnp.int32)[:, None] * t - (t - 1) + m[None, :]
    strips = _bias_table(rel_table, dist.reshape(-1, LANES))
    strips = strips.reshape(N_HEADS * (nd + 1), 1, 2 * t)
    return pl.pallas_call(
        functools.partial(_toeplitz_kernel, t=t),
        grid=(N_HEADS, nd + 1),
        in_specs=[pl.BlockSpec((1, 1, 2 * t), lambda h, dl: (h * (nd + 1) + dl, 0, 0))],
        out_specs=pl.BlockSpec((1, 1, t, t), lambda h, dl: (h, dl, 0, 0)),
        out_shape=jax.ShapeDtypeStruct((N_HEADS, nd + 1, t, t), F32),
        name="bias_toeplitz",
    )(strips)


def _sample_bias(rel_table, past, n_pages, page, dt):
    tq = jnp.arange(dt, dtype=jnp.int32)
    kpos = jnp.arange(n_pages * page, dtype=jnp.int32).reshape(n_pages, 1, page)
    dist_pages = (past + tq[None, :, None] - kpos).reshape(-1, page)
    j = jnp.arange(page, dtype=jnp.int32)
    self_rows = jnp.where(j[None, :] < dt, tq[:, None] - j[None, :], -1)
    dist = jnp.concatenate([dist_pages, self_rows], axis=0)
    rows = dist.shape[0]
    rpad = (-rows) % (256 if rows > 256 else SUBLANES)
    dist = jnp.concatenate([dist, jnp.full((rpad, page), -1, jnp.int32)], axis=0)
    bias = _bias_table(rel_table, dist)
    bp = bias[:, :n_pages * dt].reshape(N_HEADS, n_pages, dt, page)
    bp = bp.transpose(1, 2, 0, 3).reshape(n_pages, dt * N_HEADS, page)
    bs = bias[:, n_pages * dt:n_pages * dt + dt].transpose(1, 0, 2).reshape(dt * N_HEADS, page)
    return bp, bs


def kernel(x_prompt, x_sample, cache_k, cache_v, state_conv, page_table, rel_table, attn_norm_g,
           w_in, q_norm_g, k_norm_g, lambda_q, lambda_k, sub_norm_g, conv_w, w_o, ffn_norm_g,
           router_group_w, router_group_b, router_expert_w, router_expert_b, w_gate, w_up, w_down):
    depth = w_in.shape[0]
    assert depth == 1, "single-layer trunk"
    batch, seq, d = x_prompt.shape
    assert batch == 1, "one prompt sequence"
    nb, dt, _ = x_sample.shape
    n_pool, page = cache_k.shape[1], cache_k.shape[2]
    n_pages = page_table.shape[1]
    past = n_pages * page
    aw = N_HEADS * V_HEAD_DIM
    layer = 0
    lam_init = 0.8 - 0.6 * math.exp(-0.3 * layer)

    w_in_b = w_in[layer].astype(BF16)
    w_o_b = w_o[layer].astype(BF16)
    g_attn = attn_norm_g[layer].reshape(1, d)
    g_ffn = ffn_norm_g[layer].reshape(1, d)
    qg = jnp.tile(q_norm_g[layer], aw // QK_HEAD_DIM).reshape(1, aw)
    kg = jnp.tile(k_norm_g[layer], aw // QK_HEAD_DIM).reshape(1, aw)
    g_sub = sub_norm_g[layer].reshape(1, V_HEAD_DIM)
    g_sub_t = jnp.tile(sub_norm_g[layer], N_HEADS).reshape(1, aw)
    gsum, gexp = _group_matrices(aw)
    lq, lk = lambda_q[layer], lambda_k[layer]
    cw = conv_w[layer]

    xp = x_prompt.reshape(seq, d)
    xs = x_sample.reshape(nb * dt, d)

    qtp, kp, kbp, vp, vtp, gbp, up = _project(xp, g_attn, w_in_b, qg, kg, gsum, gexp)
    qts, ks, _, vs, _, gbs, us = _project(xs, g_attn, w_in_b, qg, kg, gsum, gexp)

    t = min(seq, 512)
    assert seq % t == 0
    nd = min(-(-(MAX_DISTANCE + t - 1) // t), seq // t)
    bias_p = _prompt_bias(rel_table, t, nd)
    ap = _prompt_attention(qtp, kbp, vtp, bias_p, lq, lk, g_sub, t=t, nd=nd, lam_init=lam_init)

    bias_pages, bias_self = _sample_bias(rel_table, past, n_pages, page, dt)
    lane_hc = jnp.arange(aw, dtype=jnp.int32) // QK_HEAD_DIM
    row_c = jnp.arange(2, dtype=jnp.int32)[:, None, None]
    row_h = jnp.arange(N_HEADS, dtype=jnp.int32)[None, None, :]
    sel = (lane_hc[None, None, None, :] == (row_h * 2 + row_c)[..., None])
    q3 = qts.T.reshape(nb, 1, dt, 1, aw)
    q_rows = jnp.where(sel[None], q3, jnp.zeros((), BF16)).reshape(nb, 2 * dt * N_HEADS, aw)
    k3 = ks.reshape(nb, dt, aw).astype(BF16)
    kt_new = jnp.pad(k3.transpose(0, 2, 1), ((0, 0), (0, 0), (0, page - dt)))
    v_new = jnp.pad(vs.reshape(nb, dt, aw).astype(BF16), ((0, 0), (0, page - dt), (0, 0)))
    cache_kt = cache_k[layer].transpose(0, 2, 3, 4, 1).reshape(n_pool, aw, page)
    cache_vr = cache_v[layer].reshape(n_pool, page * N_HEADS, V_HEAD_DIM)
    pages_per_step = math.gcd(n_pages, 8)
    a_s = _sample_attention(page_table, q_rows, kt_new, v_new, bias_pages, bias_self,
                            lq, lk, g_sub_t, cache_kt, cache_vr,
                            pages=pages_per_step, lam_init=lam_init)
    a_s = a_s.reshape(nb * dt, aw)

    wr = jnp.zeros((ROUTER_ROWS, d), F32)
    wr = wr.at[0:N_GROUPS].set(router_group_w[layer].T)
    wr = wr.at[EXPERT_ROW0:EXPERT_ROW0 + N_EXPERTS].set(router_expert_w[layer].T)
    wr_hi, wr_lo = _split_bf16(wr)
    w3 = jnp.concatenate([wr_hi, wr_lo, wr_hi], axis=1)
    bias_col = jnp.full((ROUTER_ROWS,), NEG_INF, F32)
    bias_col = bias_col.at[0:N_GROUPS].set(router_group_b[layer].astype(F32))
    bias_col = bias_col.at[EXPERT_ROW0:EXPERT_ROW0 + N_EXPERTS].set(
        router_expert_b[layer].reshape(-1).astype(F32)).reshape(ROUTER_ROWS, 1)
    x2p, idx_p, wt_p = _out_project(ap, gbp, up, cw, w_o_b, xp, g_ffn, w3, bias_col)
    u_ext = jnp.concatenate([state_conv[layer].astype(F32), us.reshape(nb, dt, aw)], axis=1)
    prev1 = u_ext[:, 1:1 + dt].reshape(nb * dt, aw)
    prev2 = u_ext[:, 0:dt].reshape(nb * dt, aw)
    x2s, idx_s, wt_s = _out_project(a_s, gbs, us, cw, w_o_b, xs, g_ffn, w3, bias_col,
                                    prev=(prev1, prev2))
    x2 = jnp.concatenate([x2p, x2s], axis=0)
    e_idx = jnp.concatenate([idx_p, idx_s], axis=1)
    wts = jnp.concatenate([wt_p, wt_s], axis=1)
    total = seq + nb * dt

    tm_e = 256
    plan, y_rows = _routing_plan(e_idx, wts, tm_e, d // LANES)
    y2 = _experts(*plan, g_ffn, w_gate[layer], w_up[layer], w_down[layer], x2,
                  tm=tm_e, out_rows=y_rows)
    yp = _combine(x2, y2, 0, seq, total, d)
    ys = _combine(x2, y2, seq, nb * dt, total, d)

    return (yp.reshape(batch, seq, d),
            ys.reshape(nb, dt, d),
            kp.reshape(depth, batch, seq, N_HEADS, 2, QK_HEAD_DIM),
            vp.reshape(depth, batch, seq, N_HEADS, V_HEAD_DIM),
            up[seq - (CONV_K - 1):].reshape(depth, batch, CONV_K - 1, aw),
            ks.reshape(depth, nb, dt, N_HEADS, 2, QK_HEAD_DIM),
            vs.reshape(depth, nb, dt, N_HEADS, V_HEAD_DIM),
            u_ext[:, dt:].reshape(depth, nb, CONV_K - 1, aw))
```

```python
import functools
import math

import jax
import jax.numpy as jnp
from jax import lax
from jax.experimental import pallas as pl
from jax.experimental.pallas import tpu as pltpu

N_HEADS = 8
V_HEAD_DIM = 128
QK_HEAD_DIM = 64
CONV_K = 3
N_BUCKETS = 32
MAX_EXACT = N_BUCKETS // 2
MAX_DISTANCE = 2048
N_GROUPS = 4
EXPERTS_PER_GROUP = 4
N_EXPERTS = N_GROUPS * EXPERTS_PER_GROUP
RMS_EPS = 1e-6
NEG_INF = -1e30
LOG2E = math.log2(math.e)

LANES = 128
SUBLANES = 8
VMEM_PHYSICAL_BYTES = 64 * 1024 * 1024
DMA_PRIORITIES = 2

F32 = jnp.float32
BF16 = jnp.bfloat16


def _vmem_limit(nbytes):
    return int(min(max(nbytes * 5 // 4 + (4 << 20), 16 << 20), VMEM_PHYSICAL_BYTES - (6 << 20)))


def _dot(a, b):
    return jnp.dot(a, b, preferred_element_type=F32)


def _dot_nt(a, b):
    return lax.dot_general(a, b, (((1,), (1,)), ((), ())), preferred_element_type=F32)


def _split_bf16(x):
    hi = x.astype(BF16)
    lo = (x - hi.astype(F32)).astype(BF16)
    return hi, lo


def _bias_kernel(table_ref, dist_ref, out_ref):
    d = dist_ref[...]
    n = jnp.maximum(d, 0)
    nf = jnp.maximum(n, 1).astype(F32)
    large = MAX_EXACT + (jnp.log(nf / MAX_EXACT) / math.log(MAX_DISTANCE / MAX_EXACT)
                         * (N_BUCKETS - MAX_EXACT)).astype(jnp.int32)
    large = jnp.minimum(large, N_BUCKETS - 1)
    bucket = jnp.where(n < MAX_EXACT, n, large)
    hits = [bucket == b for b in range(N_BUCKETS)]
    for h in range(N_HEADS):
        acc = jnp.zeros(d.shape, F32)
        for b in range(N_BUCKETS):
            acc = jnp.where(hits[b], table_ref[b, h] * LOG2E, acc)
        out_ref[h] = jnp.where(d < 0, NEG_INF, acc)


def _bias_table(rel_table, dist):
    rows = dist.shape[0]
    tr = min(rows, 256)
    assert rows % tr == 0
    return pl.pallas_call(
        _bias_kernel,
        grid=(rows // tr,),
        in_specs=[pl.BlockSpec(memory_space=pltpu.SMEM),
                  pl.BlockSpec((tr, LANES), lambda i: (i, 0))],
        out_specs=pl.BlockSpec((N_HEADS, tr, LANES), lambda i: (0, i, 0)),
        out_shape=jax.ShapeDtypeStruct((N_HEADS, rows, LANES), F32),
        name="rel_bias",
    )(rel_table.astype(F32), dist)


def _group_rms_scale(z, gsum, gexp):
    hi, lo = _split_bf16(z * z)
    ssq = _dot(jnp.concatenate([hi, lo], axis=1), gsum)
    r = lax.rsqrt(ssq * (1.0 / QK_HEAD_DIM) + RMS_EPS)
    r_hi, r_lo = _split_bf16(r)
    return _dot(jnp.concatenate([r_hi, r_lo], axis=1), gexp)


def _proj_kernel(x_ref, g_ref, w_ref, qg_ref, kg_ref, gsum_ref, gexp_ref,
                 qt_ref, kf_ref, kb_ref, vf_ref, vt_ref, gb_ref, u_ref, *, aw):
    x = x_ref[...]
    ms = jnp.mean(x * x, axis=-1, keepdims=True)
    xn = (x * lax.rsqrt(ms + RMS_EPS) * g_ref[...]).astype(BF16)
    gsum = gsum_ref[...]
    gexp = gexp_ref[...]

    zq = _dot(xn, w_ref[:, 0 * aw:1 * aw])
    q = zq * _group_rms_scale(zq, gsum, gexp) * qg_ref[...]
    qt_ref[...] = (q * (QK_HEAD_DIM ** -0.5 * LOG2E)).T.astype(BF16)

    zk = _dot(xn, w_ref[:, 1 * aw:2 * aw])
    k = zk * _group_rms_scale(zk, gsum, gexp) * kg_ref[...]
    kf_ref[...] = k
    kb_ref[...] = k.astype(BF16)

    v = _dot(xn, w_ref[:, 2 * aw:3 * aw])
    vf_ref[...] = v
    vt_ref[...] = v.T.astype(BF16)

    gb_ref[...] = _dot(xn, w_ref[:, 3 * aw:4 * aw])
    u_ref[...] = _dot(xn, w_ref[:, 4 * aw:5 * aw]) * _dot(xn, w_ref[:, 5 * aw:6 * aw])


def _project(x, g, w_bf16, q_gain, k_gain, gsum, gexp):
    rows, d = x.shape
    pw = w_bf16.shape[1]
    aw = pw // 6
    tm = min(rows, 256)
    assert rows % tm == 0 and aw % LANES == 0
    row_blk = lambda width: pl.BlockSpec((tm, width), lambda i: (i, 0))
    col_blk = pl.BlockSpec((aw, tm), lambda i: (0, i))
    const = lambda shape: pl.BlockSpec(shape, lambda i: (0, 0))
    vmem = (d * pw * 2 + 2 * tm * d * 4 + 2 * tm * aw * (2 + 4 + 2 + 4 + 2 + 4 + 4)
            + 8 * tm * aw * 4 + 4 * aw * (2 * LANES + 2 * LANES))
    return pl.pallas_call(
        functools.partial(_proj_kernel, aw=aw),
        grid=(rows // tm,),
        in_specs=[row_blk(d), const((1, d)),
                  pl.BlockSpec((d, pw), lambda i: (0, 0), pipeline_mode=pl.Buffered(1)),
                  const((1, aw)), const((1, aw)),
                  const((2 * aw, LANES)), const((2 * LANES, aw))],
        out_specs=[col_blk, row_blk(aw), row_blk(aw), row_blk(aw), col_blk,
                   row_blk(aw), row_blk(aw)],
        out_shape=[jax.ShapeDtypeStruct((aw, rows), BF16),
                   jax.ShapeDtypeStruct((rows, aw), F32),
                   jax.ShapeDtypeStruct((rows, aw), BF16),
                   jax.ShapeDtypeStruct((rows, aw), F32),
                   jax.ShapeDtypeStruct((aw, rows), BF16),
                   jax.ShapeDtypeStruct((rows, aw), F32),
                   jax.ShapeDtypeStruct((rows, aw), F32)],
        compiler_params=pltpu.CompilerParams(
            dimension_semantics=("arbitrary",), vmem_limit_bytes=_vmem_limit(vmem)),
        name="in_proj",
    )(x, g, w_bf16, q_gain, k_gain, gsum, gexp)


def _lambda(lq_ref, lk_ref, lam_init):
    e = jnp.exp(jnp.sum(lq_ref[...] * lk_ref[...], axis=-1, keepdims=True))
    return e[0:1] - e[1:2] + lam_init


def _prompt_attn_kernel(qt_ref, k_ref, vt_ref, bias_ref, lq_ref, lk_ref, gs_ref,
                        o_ref, m_scr, l_scr, acc_scr, sa_scr, sb_scr, *, t, nd, lam_init):
    qi = pl.program_id(1)
    qt = qt_ref[...]
    row = lax.broadcasted_iota(jnp.int32, qt.shape, 0)
    zero = jnp.zeros_like(qt)
    qmaps = (jnp.where(row < QK_HEAD_DIM, qt, zero), jnp.where(row >= QK_HEAD_DIM, qt, zero))
    m_scr[...] = jnp.full(m_scr.shape, NEG_INF, F32)
    l_scr[...] = jnp.zeros(l_scr.shape, F32)
    acc_scr[...] = jnp.zeros(acc_scr.shape, F32)

    def scores(ki, s_scr):
        kb = k_ref[pl.ds(pl.multiple_of(ki * t, t), t), :]
        for c in range(2):
            s_scr[c] = _dot(kb, qmaps[c])

    def attend(ki, s_scr, delta):
        vt = vt_ref[:, pl.ds(pl.multiple_of(ki * t, t), t)]
        b = bias_ref[0, delta]
        for c in range(2):
            s = s_scr[c] + b
            m_prev = m_scr[c]
            m_new = jnp.maximum(m_prev, jnp.max(s, axis=0, keepdims=True))
            p = jnp.exp2(s - m_new)
            alpha = jnp.exp2(m_prev - m_new)
            l_scr[c] = alpha * l_scr[c] + jnp.sum(p, axis=0, keepdims=True)
            acc_scr[c] = alpha * acc_scr[c] + _dot(vt, p.astype(BF16))
            m_scr[c] = m_new

    scores(0, sa_scr)

    def body(j, carry):
        k0 = 2 * j
        scores(k0 + 1, sb_scr)
        attend(k0, sa_scr, jnp.minimum(qi - k0, nd))
        scores(k0 + 2, sa_scr)
        attend(k0 + 1, sb_scr, jnp.minimum(qi - k0 - 1, nd))
        return carry

    lax.fori_loop(0, qi // 2, body, 0)

    @pl.when(qi % 2 == 0)
    def _():
        attend(qi, sa_scr, 0)

    @pl.when(qi % 2 == 1)
    def _():
        scores(qi, sb_scr)
        attend(qi - 1, sa_scr, min(1, nd))
        attend(qi, sb_scr, 0)

    lam = _lambda(lq_ref, lk_ref, lam_init)
    o = acc_scr[0] / l_scr[0] - lam * (acc_scr[1] / l_scr[1])
    o = o * lax.rsqrt(jnp.mean(o * o, axis=0, keepdims=True) + RMS_EPS)
    o_ref[...] = (o.T * gs_ref[...] * (1.0 - lam_init)).astype(o_ref.dtype)


def _prompt_attention(qt, kb, vt, bias, lq, lk, g_sub, *, t, nd, lam_init):
    aw, s = qt.shape
    vmem = (2 * t * LANES * 2 + 2 * LANES * s * 2 + 2 * s * LANES * 2
            + 2 * (nd + 1) * t * t * 4 + 2 * t * LANES * 2
            + 2 * t * (2 * SUBLANES + LANES) * 4 + 4 * t * t * 4 + 8 * t * t * 4)
    return pl.pallas_call(
        functools.partial(_prompt_attn_kernel, t=t, nd=nd, lam_init=lam_init),
        grid=(N_HEADS, s // t),
        in_specs=[pl.BlockSpec((V_HEAD_DIM, t), lambda h, i: (h, i)),
                  pl.BlockSpec((s, V_HEAD_DIM), lambda h, i: (0, h)),
                  pl.BlockSpec((V_HEAD_DIM, s), lambda h, i: (h, 0)),
                  pl.BlockSpec((1, nd + 1, t, t), lambda h, i: (h, 0, 0, 0)),
                  pl.BlockSpec((2, QK_HEAD_DIM), lambda h, i: (0, 0)),
                  pl.BlockSpec((2, QK_HEAD_DIM), lambda h, i: (0, 0)),
                  pl.BlockSpec((1, V_HEAD_DIM), lambda h, i: (0, 0))],
        out_specs=pl.BlockSpec((t, V_HEAD_DIM), lambda h, i: (i, h)),
        out_shape=jax.ShapeDtypeStruct((s, aw), BF16),
        scratch_shapes=[pltpu.VMEM((2, 1, t), F32), pltpu.VMEM((2, 1, t), F32),
                        pltpu.VMEM((2, V_HEAD_DIM, t), F32),
                        pltpu.VMEM((2, t, t), F32), pltpu.VMEM((2, t, t), F32)],
        compiler_params=pltpu.CompilerParams(
            dimension_semantics=("arbitrary", "arbitrary"),
            vmem_limit_bytes=_vmem_limit(vmem)),
        name="prompt_attn",
    )(qt, kb, vt, bias, lq, lk, g_sub)


def _sample_attn_kernel(pt_ref, qr_ref, kn_ref, vn_ref, bias_ref, bself_ref, lq_ref, lk_ref,
                        gs_ref, *rest, pages, page, lam_init):
    del pt_ref
    k_refs = rest[:pages]
    v_refs = rest[pages:2 * pages]
    o_ref, m_scr, l_scr, acc_scr = rest[2 * pages:]
    step = pl.program_id(1)
    qr = qr_ref[0]
    half = qr.shape[0] // 2

    def both_maps(b):
        return jnp.concatenate([b, b], axis=0)

    def update(s, vs):
        m_prev = m_scr[...]
        m_new = jnp.maximum(m_prev, jnp.max(s, axis=-1, keepdims=True))
        p = jnp.exp2(s - m_new)
        alpha = jnp.exp2(m_prev - m_new)
        l_scr[...] = alpha * l_scr[...] + jnp.sum(p, axis=-1, keepdims=True)
        p = p.astype(BF16)
        acc = alpha * acc_scr[...]
        for j, vj in enumerate(vs):
            acc = acc + _dot(p[:, j * page:(j + 1) * page], vj)
        acc_scr[...] = acc
        m_scr[...] = m_new

    @pl.when(step == 0)
    def _():
        m_scr[...] = jnp.full(m_scr.shape, NEG_INF, F32)
        l_scr[...] = jnp.zeros(l_scr.shape, F32)
        acc_scr[...] = jnp.zeros(acc_scr.shape, F32)
        update(_dot(qr, kn_ref[0]) + both_maps(bself_ref[...]), [vn_ref[0]])

    def v_page(ref):
        return jnp.concatenate(
            [ref[0, pl.ds(h, page, stride=N_HEADS), :] for h in range(N_HEADS)],
            axis=1).astype(BF16)

    s = jnp.concatenate(
        [_dot(qr, k_refs[j][0].astype(BF16)) + both_maps(bias_ref[j]) for j in range(pages)],
        axis=1)
    update(s, [v_page(v_refs[j]) for j in range(pages)])

    @pl.when(step == pl.num_programs(1) - 1)
    def _():
        o = acc_scr[...] / l_scr[...]
        lam = _lambda(lq_ref, lk_ref, lam_init)
        o = o[:half] - lam * o[half:]
        row_head = lax.broadcasted_iota(jnp.int32, o.shape, 0) % N_HEADS
        lane_head = lax.broadcasted_iota(jnp.int32, o.shape, 1) // V_HEAD_DIM
        o = jnp.where(row_head == lane_head, o, 0.0)
        ms = jnp.sum(o * o, axis=-1, keepdims=True) * (1.0 / V_HEAD_DIM)
        o = o * lax.rsqrt(ms + RMS_EPS)
        o = jnp.sum(o.reshape(half // N_HEADS, N_HEADS, o.shape[1]), axis=1)
        o_ref[0] = (o * gs_ref[...] * (1.0 - lam_init)).astype(o_ref.dtype)


def _sample_attention(page_table, q_rows, kt_new, v_new, bias_pages, bias_self, lq, lk, g_sub_t,
                      cache_kt, cache_vr, *, pages, lam_init):
    nb, n_pages = page_table.shape
    _, aw, page = cache_kt.shape
    dt = q_rows.shape[1] // (2 * N_HEADS)
    steps = n_pages // pages
    assert n_pages % pages == 0
    rows = q_rows.shape[1]

    def page_spec(shape):
        return [pl.BlockSpec((1,) + shape,
                             lambda b, s, pt, j=j: (pt[b * n_pages + s * pages + j], 0, 0))
                for j in range(pages)]

    per_b = lambda shape: pl.BlockSpec(shape, lambda b, s, pt: (b, 0, 0))
    const2 = lambda shape: pl.BlockSpec(shape, lambda b, s, pt: (0, 0))
    vmem = (2 * 2 * pages * page * aw * 4 + 3 * pages * page * aw * 2
            + 4 * rows * aw * 4 + 2 * 3 * rows * aw * 2)
    grid_spec = pltpu.PrefetchScalarGridSpec(
        num_scalar_prefetch=1,
        grid=(nb, steps),
        in_specs=[per_b((1, rows, aw)), per_b((1, aw, page)), per_b((1, page, aw)),
                  pl.BlockSpec((pages, rows // 2, page), lambda b, s, pt: (s, 0, 0)),
                  const2((rows // 2, page)),
                  const2((2, QK_HEAD_DIM)), const2((2, QK_HEAD_DIM)), const2((1, aw))]
                 + page_spec((aw, page)) + page_spec((page * N_HEADS, V_HEAD_DIM)),
        out_specs=pl.BlockSpec((1, dt, aw), lambda b, s, pt: (b, 0, 0)),
        scratch_shapes=[pltpu.VMEM((rows, 1), F32), pltpu.VMEM((rows, 1), F32),
                        pltpu.VMEM((rows, aw), F32)])
    return pl.pallas_call(
        functools.partial(_sample_attn_kernel, pages=pages, page=page, lam_init=lam_init),
        grid_spec=grid_spec,
        out_shape=jax.ShapeDtypeStruct((nb, dt, aw), BF16),
        compiler_params=pltpu.CompilerParams(
            dimension_semantics=("arbitrary", "arbitrary"),
            vmem_limit_bytes=_vmem_limit(vmem)),
        name="sample_attn",
    )(page_table.reshape(-1), q_rows, kt_new, v_new, bias_pages, bias_self, lq, lk, g_sub_t,
      *([cache_kt] * pages), *([cache_vr] * pages))


def _out_proj_kernel(*refs, halo):
    if halo:
        a_ref, gb_ref, u_ref, prev_ref, cw_ref, wo_ref, x_ref, o_ref = refs
        u = u_ref[...]
        first = pl.program_id(0) == 0
        prev = jnp.where(first, 0.0, prev_ref[...])
        ucat = jnp.concatenate([prev, u], axis=0)
        p1 = pltpu.roll(ucat, 1, axis=0)[SUBLANES:]
        p2 = pltpu.roll(ucat, 2, axis=0)[SUBLANES:]
    else:
        a_ref, gb_ref, u_ref, p1_ref, p2_ref, cw_ref, wo_ref, x_ref, o_ref = refs
        u, p1, p2 = u_ref[...], p1_ref[...], p2_ref[...]
    cw = cw_ref[...]
    aw = u.shape[1]
    c = gb_ref[...] * (p2 * cw[0:1] + p1 * cw[1:2] + u * cw[2:3])
    y = _dot(a_ref[...], wo_ref[:aw, :]) + _dot(c.astype(BF16), wo_ref[aw:, :])
    o_ref[...] = x_ref[...] + y


def _out_project(a, gb, u, conv_w, wo_bf16, x, prev=None):
    rows, d = x.shape
    aw = a.shape[1]
    tm = min(rows, 256)
    assert rows % tm == 0
    row_blk = lambda width: pl.BlockSpec((tm, width), lambda i: (i, 0))
    const = lambda shape: pl.BlockSpec(shape, lambda i: (0, 0))
    halo = prev is None
    if halo:
        nsub = tm // SUBLANES
        extra = [pl.BlockSpec((SUBLANES, aw), lambda i: (jnp.maximum(i * nsub - 1, 0), 0))]
        extra_args = [u]
    else:
        extra = [row_blk(aw), row_blk(aw)]
        extra_args = list(prev)
    vmem = 2 * d * d * 2 + 2 * tm * (aw * (2 + 4 + 4 + 8) + d * 8) + 6 * tm * d * 4
    return pl.pallas_call(
        functools.partial(_out_proj_kernel, halo=halo),
        grid=(rows // tm,),
        in_specs=[row_blk(aw), row_blk(aw), row_blk(aw)] + extra
                 + [const((CONV_K, aw)), const((d, d)), row_blk(d)],
        out_specs=row_blk(d),
        out_shape=jax.ShapeDtypeStruct((rows, d), F32),
        compiler_params=pltpu.CompilerParams(
            dimension_semantics=("arbitrary",), vmem_limit_bytes=_vmem_limit(vmem)),
        name="out_proj",
    )(a, gb, u, *extra_args, conv_w, wo_bf16, x)


ROUTER_ROWS = 32
EXPERT_ROW0 = 8


def _router_kernel(x_ref, g_ref, w3_ref, bias_ref, idx_ref, wt_ref):
    x = x_ref[...]
    ms = jnp.mean(x * x, axis=-1, keepdims=True)
    xn = x * lax.rsqrt(ms + RMS_EPS) * g_ref[...]
    hi, lo = _split_bf16(xn)
    logits = _dot_nt(w3_ref[...], jnp.concatenate([hi, hi, lo], axis=1)) + bias_ref[...]
    big = jnp.int32(1 << 20)

    gl = logits[0:SUBLANES]
    grow = lax.broadcasted_iota(jnp.int32, gl.shape, 0)
    gmax = jnp.max(gl, axis=0, keepdims=True)
    g_idx = jnp.min(jnp.where(gl == gmax, grow, big), axis=0, keepdims=True)
    g_w = 1.0 / jnp.sum(jnp.exp(gl - gmax), axis=0, keepdims=True)

    el = logits[EXPERT_ROW0:EXPERT_ROW0 + N_EXPERTS]
    erow = lax.broadcasted_iota(jnp.int32, el.shape, 0)
    el = jnp.where(erow // EXPERTS_PER_GROUP == g_idx, el, NEG_INF)
    v1 = jnp.max(el, axis=0, keepdims=True)
    i1 = jnp.min(jnp.where(el == v1, erow, big), axis=0, keepdims=True)
    el2 = jnp.where(erow == i1, NEG_INF, el)
    v2 = jnp.max(el2, axis=0, keepdims=True)
    i2 = jnp.min(jnp.where(el2 == v2, erow, big), axis=0, keepdims=True)
    t = jnp.exp(v2 - v1)
    w1 = g_w / (1.0 + t)
    idx_ref[...] = jnp.concatenate([i1, i2], axis=0)
    wt_ref[...] = jnp.concatenate([w1, w1 * t], axis=0)


def _route(x, g, w3, bias_col):
    rows, d = x.shape
    tm = math.gcd(rows, 256)
    assert tm % LANES == 0
    return pl.pallas_call(
        _router_kernel,
        grid=(rows // tm,),
        in_specs=[pl.BlockSpec((tm, d), lambda i: (i, 0)),
                  pl.BlockSpec((1, d), lambda i: (0, 0)),
                  pl.BlockSpec((ROUTER_ROWS, 3 * d), lambda i: (0, 0)),
                  pl.BlockSpec((ROUTER_ROWS, 1), lambda i: (0, 0))],
        out_specs=[pl.BlockSpec((2, tm), lambda i: (0, i)),
                   pl.BlockSpec((2, tm), lambda i: (0, i))],
        out_shape=[jax.ShapeDtypeStruct((2, rows), jnp.int32),
                   jax.ShapeDtypeStruct((2, rows), F32)],
        compiler_params=pltpu.CompilerParams(dimension_semantics=("arbitrary",)),
        name="router",
    )(x, g, w3, bias_col)


EXPERT_PHASES = 4


def _expert_kernel(te_ref, src_cur, src_next, dst_prev, dst_cur, roww_ref, g_ref,
                   wg_ref, wu_ref, wd_ref, x_hbm, y_hbm,
                   xbuf, ybuf, wgb, wub, wdb, gsem, ssem, *, tm):
    j = pl.program_id(0)
    last = pl.num_programs(0) - 1
    slot = j % 2
    other = 1 - slot

    def gather_rows(idx_ref, buf, lo, hi):
        for r in range(lo, hi):
            pltpu.make_async_copy(x_hbm.at[pl.ds(idx_ref[0, 0, r], 1)],
                                  xbuf.at[buf, pl.ds(r, 1)], gsem.at[buf]).start(
                                      priority=r % DMA_PRIORITIES)

    def scatter_rows(idx_ref, buf, lo, hi):
        for r in range(lo, hi):
            pltpu.make_async_copy(ybuf.at[buf, pl.ds(r, 1)],
                                  y_hbm.at[pl.ds(idx_ref[0, 0, r], 1)], ssem.at[buf]).start(
                                      priority=r % DMA_PRIORITIES)

    def wait_gather(buf):
        pltpu.make_async_copy(x_hbm.at[pl.ds(0, tm)], xbuf.at[buf], gsem.at[buf]).wait()

    def wait_scatter(buf):
        pltpu.make_async_copy(ybuf.at[buf], y_hbm.at[pl.ds(0, tm)], ssem.at[buf]).wait()

    @pl.when(j == 0)
    def _():
        ybuf[1] = jnp.zeros(ybuf.shape[1:], F32)
        gather_rows(src_cur, 0, 0, tm)

    @pl.when(jnp.logical_or(j == 0, te_ref[j] != te_ref[jnp.maximum(j - 1, 0)]))
    def _():
        wgb[...] = wg_ref[0].astype(BF16)
        wub[...] = wu_ref[0].astype(BF16)
        wdb[...] = wd_ref[0].astype(BF16)

    wait_gather(slot)

    @pl.when(j >= 1)
    def _():
        wait_scatter(slot)

    x = xbuf[slot]
    ms = jnp.mean(x * x, axis=-1, keepdims=True)
    xn = (x * lax.rsqrt(ms + RMS_EPS) * g_ref[...]).astype(BF16)
    roww = roww_ref[...]
    ff = wgb.shape[1]
    d = wdb.shape[1]
    step = tm // EXPERT_PHASES
    half_ff, half_d = ff // 2, d // 2
    hs = []
    for c in range(2):
        gather_rows(src_next, other, (2 * c) * step, (2 * c + 2) * step)
        cols = slice(c * half_ff, (c + 1) * half_ff)
        hg = _dot(xn, wgb[:, cols])
        hu = _dot(xn, wub[:, cols])
        hs.append((hg * jax.nn.sigmoid(hg) * hu * roww).astype(BF16))
    h = jnp.concatenate(hs, axis=1)
    for c in range(2):
        scatter_rows(dst_prev, other, (2 * c) * step, (2 * c + 2) * step)
        cols = slice(c * half_d, (c + 1) * half_d)
        ybuf[slot, :, cols] = _dot(h, wdb[:, cols])

    @pl.when(j == last)
    def _():
        scatter_rows(dst_cur, slot, 0, tm)
        wait_gather(other)
        wait_scatter(other)
        wait_scatter(slot)


def _experts(tile_expert, src_cur, src_next, dst_prev, dst_cur, row_w, g, w_gate, w_up, w_down,
             x, *, tm, out_rows):
    nt = tile_expert.shape[0]
    d = x.shape[1]
    ff = w_gate.shape[2]
    assert tm % EXPERT_PHASES == 0
    smem_row = pl.BlockSpec((1, 1, tm), lambda i, te: (i, 0, 0), memory_space=pltpu.SMEM)
    expert_w = lambda shape: pl.BlockSpec((1,) + shape, lambda i, te: (te[i], 0, 0))
    grid_spec = pltpu.PrefetchScalarGridSpec(
        num_scalar_prefetch=1,
        grid=(nt,),
        in_specs=[smem_row, smem_row, smem_row, smem_row,
                  pl.BlockSpec((tm, 1), lambda i, te: (i, 0)),
                  pl.BlockSpec((1, d), lambda i, te: (0, 0)),
                  expert_w((d, ff)), expert_w((d, ff)), expert_w((ff, d)),
                  pl.BlockSpec(memory_space=pl.ANY)],
        out_specs=pl.BlockSpec(memory_space=pl.ANY),
        scratch_shapes=[pltpu.VMEM((2, tm, d), F32), pltpu.VMEM((2, tm, d), F32),
                        pltpu.VMEM((d, ff), BF16), pltpu.VMEM((d, ff), BF16),
                        pltpu.VMEM((ff, d), BF16),
                        pltpu.SemaphoreType.DMA((2,)), pltpu.SemaphoreType.DMA((2,))])
    vmem = 2 * 3 * d * ff * 4 + 3 * d * ff * 2 + 4 * tm * d * 4 + 6 * tm * d * 4
    return pl.pallas_call(
        functools.partial(_expert_kernel, tm=tm),
        grid_spec=grid_spec,
        out_shape=jax.ShapeDtypeStruct((out_rows, d), F32),
        compiler_params=pltpu.CompilerParams(
            dimension_semantics=("arbitrary",), vmem_limit_bytes=_vmem_limit(vmem)),
        name="experts",
    )(tile_expert, src_cur, src_next, dst_prev, dst_cur, row_w, g, w_gate, w_up, w_down, x)


def _combine_kernel(x_ref, y0_ref, y1_ref, o_ref):
    o_ref[...] = x_ref[...] + (y0_ref[...] + y1_ref[...])


def _combine(x2, y2, row0, rows, total_rows):
    d = x2.shape[1]
    tm = math.gcd(math.gcd(rows, row0) if row0 else rows, LANES)
    assert total_rows % tm == 0 and tm % SUBLANES == 0
    b0 = row0 // tm
    b1 = (total_rows + row0) // tm
    return pl.pallas_call(
        _combine_kernel,
        grid=(rows // tm,),
        in_specs=[pl.BlockSpec((tm, d), lambda i: (b0 + i, 0)),
                  pl.BlockSpec((tm, d), lambda i: (b0 + i, 0)),
                  pl.BlockSpec((tm, d), lambda i: (b1 + i, 0))],
        out_specs=pl.BlockSpec((tm, d), lambda i: (i, 0)),
        out_shape=jax.ShapeDtypeStruct((rows, d), F32),
        compiler_params=pltpu.CompilerParams(dimension_semantics=("arbitrary",)),
        name="moe_combine",
    )(x2, y2, y2)


def _routing_plan(e_idx, w, tm):
    total = e_idx.shape[1]
    n_assign = 2 * total
    nt = -(-n_assign // tm) + N_EXPERTS
    e_flat = e_idx.reshape(-1)
    w_flat = w.reshape(-1)
    order = jnp.argsort(e_flat, stable=True).astype(jnp.int32)
    counts = jnp.sum((e_flat[:, None] == jnp.arange(N_EXPERTS, dtype=jnp.int32)[None, :])
                     .astype(jnp.int32), axis=0)
    tiles_per = (counts + tm - 1) // tm
    tile_end = jnp.cumsum(tiles_per)
    tile_start = tile_end - tiles_per
    group_start = jnp.cumsum(counts) - counts
    tiles = jnp.arange(nt, dtype=jnp.int32)
    tile_expert = jnp.minimum(
        jnp.sum((tiles[:, None] >= tile_end[None, :]).astype(jnp.int32), axis=1), N_EXPERTS - 1)
    tile_valid = (tiles < tile_end[-1]).astype(jnp.int32)
    rank = (tiles - tile_start[tile_expert])[:, None] * tm + jnp.arange(tm, dtype=jnp.int32)[None, :]
    valid = (rank < counts[tile_expert][:, None]) & (tile_valid[:, None] > 0)
    pos = jnp.clip(group_start[tile_expert][:, None] + rank, 0, n_assign - 1)
    assign = order[pos]
    pad_rank = jnp.cumsum(jnp.logical_not(valid).reshape(-1).astype(jnp.int32)) - 1
    dump = n_assign + pad_rank.reshape(nt, tm)
    row_src = jnp.where(valid, assign % total, 0)
    row_dst = jnp.where(valid, assign, dump)
    row_w = jnp.where(valid, w_flat[assign], 0.0)
    src_next = jnp.concatenate([row_src[1:], row_src[-1:]], axis=0)
    first_dst = nt * tm + jnp.arange(tm, dtype=jnp.int32)[None, :]
    dst_prev = jnp.concatenate([first_dst, row_dst[:-1]], axis=0)
    as_smem = lambda a: a.reshape(nt, 1, tm)
    return (tile_expert, as_smem(row_src), as_smem(src_next), as_smem(dst_prev), as_smem(row_dst),
            row_w.reshape(nt * tm, 1)), (nt + 1) * tm


def _group_matrices(aw):
    e = jnp.arange(aw, dtype=jnp.int32) // QK_HEAD_DIM
    lanes = jnp.arange(LANES, dtype=jnp.int32)
    member = (e[:, None] == lanes[None, :]).astype(BF16)
    return jnp.concatenate([member, member], axis=0), jnp.concatenate([member.T, member.T], axis=0)


def _toeplitz_kernel(strip_ref, out_ref, *, t):
    x = jnp.broadcast_to(strip_ref[0], (t, 2 * t))
    out_ref[0, 0] = pltpu.roll(x, t + 1, axis=1, stride=1, stride_axis=0)[:, :t]


def _prompt_bias(rel_table, t, nd):
    m = jnp.arange(2 * t, dtype=jnp.int32)
    dist = jnp.arange(nd + 1, dtype=jnp.int32)[:, None] * t - (t - 1) + m[None, :]
    strips = _bias_table(rel_table, dist.reshape(-1, LANES))
    strips = strips.reshape(N_HEADS * (nd + 1), 1, 2 * t)
    return pl.pallas_call(
        functools.partial(_toeplitz_kernel, t=t),
        grid=(N_HEADS, nd + 1),
        in_specs=[pl.BlockSpec((1, 1, 2 * t), lambda h, dl: (h * (nd + 1) + dl, 0, 0))],
        out_specs=pl.BlockSpec((1, 1, t, t), lambda h, dl: (h, dl, 0, 0)),
        out_shape=jax.ShapeDtypeStruct((N_HEADS, nd + 1, t, t), F32),
        name="bias_toeplitz",
    )(strips)


def _sample_bias(rel_table, past, n_pages, page, dt):
    tq = jnp.arange(dt, dtype=jnp.int32)
    kpos = jnp.arange(n_pages * page, dtype=jnp.int32).reshape(n_pages, 1, page)
    dist_pages = (past + tq[None, :, None] - kpos).reshape(-1, page)
    j = jnp.arange(page, dtype=jnp.int32)
    self_rows = jnp.where(j[None, :] < dt, tq[:, None] - j[None, :], -1)
    dist = jnp.concatenate([dist_pages, self_rows], axis=0)
    rows = dist.shape[0]
    rpad = (-rows) % (256 if rows > 256 else SUBLANES)
    dist = jnp.concatenate([dist, jnp.full((rpad, page), -1, jnp.int32)], axis=0)
    bias = _bias_table(rel_table, dist)
    bp = bias[:, :n_pages * dt].reshape(N_HEADS, n_pages, dt, page)
    bp = bp.transpose(1, 2, 0, 3).reshape(n_pages, dt * N_HEADS, page)
    bs = bias[:, n_pages * dt:n_pages * dt + dt].transpose(1, 0, 2).reshape(dt * N_HEADS, page)
    return bp, bs


def kernel(x_prompt, x_sample, cache_k, cache_v, state_conv, page_table, rel_table, attn_norm_g,
           w_in, q_norm_g, k_norm_g, lambda_q, lambda_k, sub_norm_g, conv_w, w_o, ffn_norm_g,
           router_group_w, router_group_b, router_expert_w, router_expert_b, w_gate, w_up, w_down):
    depth = w_in.shape[0]
    assert depth == 1, "single-layer trunk"
    batch, seq, d = x_prompt.shape
    assert batch == 1, "one prompt sequence"
    nb, dt, _ = x_sample.shape
    n_pool, page = cache_k.shape[1], cache_k.shape[2]
    n_pages = page_table.shape[1]
    past = n_pages * page
    aw = N_HEADS * V_HEAD_DIM
    layer = 0
    lam_init = 0.8 - 0.6 * math.exp(-0.3 * layer)

    w_in_b = w_in[layer].astype(BF16)
    w_o_b = w_o[layer].astype(BF16)
    g_attn = attn_norm_g[layer].reshape(1, d)
    g_ffn = ffn_norm_g[layer].reshape(1, d)
    qg = jnp.tile(q_norm_g[layer], aw // QK_HEAD_DIM).reshape(1, aw)
    kg = jnp.tile(k_norm_g[layer], aw // QK_HEAD_DIM).reshape(1, aw)
    g_sub = sub_norm_g[layer].reshape(1, V_HEAD_DIM)
    g_sub_t = jnp.tile(sub_norm_g[layer], N_HEADS).reshape(1, aw)
    gsum, gexp = _group_matrices(aw)
    lq, lk = lambda_q[layer], lambda_k[layer]
    cw = conv_w[layer]

    xp = x_prompt.reshape(seq, d)
    xs = x_sample.reshape(nb * dt, d)

    qtp, kp, kbp, vp, vtp, gbp, up = _project(xp, g_attn, w_in_b, qg, kg, gsum, gexp)
    qts, ks, _, vs, _, gbs, us = _project(xs, g_attn, w_in_b, qg, kg, gsum, gexp)

    t = min(seq, 512)
    assert seq % t == 0
    nd = min(-(-(MAX_DISTANCE + t - 1) // t), seq // t)
    bias_p = _prompt_bias(rel_table, t, nd)
    ap = _prompt_attention(qtp, kbp, vtp, bias_p, lq, lk, g_sub, t=t, nd=nd, lam_init=lam_init)

    bias_pages, bias_self = _sample_bias(rel_table, past, n_pages, page, dt)
    lane_hc = jnp.arange(aw, dtype=jnp.int32) // QK_HEAD_DIM
    row_c = jnp.arange(2, dtype=jnp.int32)[:, None, None]
    row_h = jnp.arange(N_HEADS, dtype=jnp.int32)[None, None, :]
    sel = (lane_hc[None, None, None, :] == (row_h * 2 + row_c)[..., None])
    q3 = qts.T.reshape(nb, 1, dt, 1, aw)
    q_rows = jnp.where(sel[None], q3, jnp.zeros((), BF16)).reshape(nb, 2 * dt * N_HEADS, aw)
    k3 = ks.reshape(nb, dt, aw).astype(BF16)
    kt_new = jnp.pad(k3.transpose(0, 2, 1), ((0, 0), (0, 0), (0, page - dt)))
    v_new = jnp.pad(vs.reshape(nb, dt, aw).astype(BF16), ((0, 0), (0, page - dt), (0, 0)))
    cache_kt = cache_k[layer].transpose(0, 2, 3, 4, 1).reshape(n_pool, aw, page)
    cache_vr = cache_v[layer].reshape(n_pool, page * N_HEADS, V_HEAD_DIM)
    pages_per_step = math.gcd(n_pages, 8)
    a_s = _sample_attention(page_table, q_rows, kt_new, v_new, bias_pages, bias_self,
                            lq, lk, g_sub_t, cache_kt, cache_vr,
                            pages=pages_per_step, lam_init=lam_init)
    a_s = a_s.reshape(nb * dt, aw)

    x2p = _out_project(ap, gbp, up, cw, w_o_b, xp)
    u_ext = jnp.concatenate([state_conv[layer].astype(F32), us.reshape(nb, dt, aw)], axis=1)
    prev1 = u_ext[:, 1:1 + dt].reshape(nb * dt, aw)
    prev2 = u_ext[:, 0:dt].reshape(nb * dt, aw)
    x2s = _out_project(a_s, gbs, us, cw, w_o_b, xs, prev=(prev1, prev2))
    x2 = jnp.concatenate([x2p, x2s], axis=0)
    total = seq + nb * dt

    wr = jnp.zeros((ROUTER_ROWS, d), F32)
    wr = wr.at[0:N_GROUPS].set(router_group_w[layer].T)
    wr = wr.at[EXPERT_ROW0:EXPERT_ROW0 + N_EXPERTS].set(router_expert_w[layer].T)
    wr_hi, wr_lo = _split_bf16(wr)
    w3 = jnp.concatenate([wr_hi, wr_lo, wr_hi], axis=1)
    bias_col = jnp.full((ROUTER_ROWS,), NEG_INF, F32)
    bias_col = bias_col.at[0:N_GROUPS].set(router_group_b[layer].astype(F32))
    bias_col = bias_col.at[EXPERT_ROW0:EXPERT_ROW0 + N_EXPERTS].set(
        router_expert_b[layer].reshape(-1).astype(F32)).reshape(ROUTER_ROWS, 1)
    e_idx, wts = _route(x2, g_ffn, w3, bias_col)
    tm_e = 256
    plan, y_rows = _routing_plan(e_idx, wts, tm_e)
    y2 = _experts(*plan, g_ffn, w_gate[layer], w_up[layer], w_down[layer], x2,
                  tm=tm_e, out_rows=y_rows)
    yp = _combine(x2, y2, 0, seq, total)
    ys = _combine(x2, y2, seq, nb * dt, total)

    return (yp.reshape(batch, seq, d),
            ys.reshape(nb, dt, d),
            kp.reshape(depth, batch, seq, N_HEADS, 2, QK_HEAD_DIM),
            vp.reshape(depth, batch, seq, N_HEADS, V_HEAD_DIM),
            up[seq - (CONV_K - 1):].reshape(depth, batch, CONV_K - 1, aw),
            ks.reshape(depth, nb, dt, N_HEADS, 2, QK_HEAD_DIM),
            vs.reshape(depth, nb, dt, N_HEADS, V_HEAD_DIM),
            u_ext[:, dt:].reshape(depth, nb, CONV_K - 1, aw))
```

```python
import functools
import math

import jax
import jax.numpy as jnp
from jax import lax
from jax.experimental import pallas as pl
from jax.experimental.pallas import tpu as pltpu

N_HEADS = 8
V_HEAD_DIM = 128
QK_HEAD_DIM = 64
CONV_K = 3
N_BUCKETS = 32
MAX_EXACT = N_BUCKETS // 2
MAX_DISTANCE = 2048
N_GROUPS = 4
EXPERTS_PER_GROUP = 4
N_EXPERTS = N_GROUPS * EXPERTS_PER_GROUP
RMS_EPS = 1e-6
NEG_INF = -1e30
LOG2E = math.log2(math.e)

LANES = 128
SUBLANES = 8
VMEM_PHYSICAL_BYTES = 64 * 1024 * 1024
PAGES_PER_STEP = 16

F32 = jnp.float32
BF16 = jnp.bfloat16


def _vmem_limit(nbytes):
    return int(min(max(nbytes * 5 // 4 + (4 << 20), 16 << 20), VMEM_PHYSICAL_BYTES - (6 << 20)))


def _dot(a, b):
    return jnp.dot(a, b, preferred_element_type=F32)


def _dot_nt(a, b):
    return lax.dot_general(a, b, (((1,), (1,)), ((), ())), preferred_element_type=F32)


def _split_bf16(x):
    hi = x.astype(BF16)
    lo = (x - hi.astype(F32)).astype(BF16)
    return hi, lo


def _bias_kernel(table_ref, dist_ref, out_ref):
    d = dist_ref[...]
    n = jnp.maximum(d, 0)
    nf = jnp.maximum(n, 1).astype(F32)
    large = MAX_EXACT + (jnp.log(nf / MAX_EXACT) / math.log(MAX_DISTANCE / MAX_EXACT)
                         * (N_BUCKETS - MAX_EXACT)).astype(jnp.int32)
    large = jnp.minimum(large, N_BUCKETS - 1)
    bucket = jnp.where(n < MAX_EXACT, n, large)
    hits = [bucket == b for b in range(N_BUCKETS)]
    for h in range(N_HEADS):
        acc = jnp.zeros(d.shape, F32)
        for b in range(N_BUCKETS):
            acc = jnp.where(hits[b], table_ref[b, h] * LOG2E, acc)
        out_ref[h] = jnp.where(d < 0, NEG_INF, acc)


def _bias_table(rel_table, dist):
    rows = dist.shape[0]
    tr = min(rows, 256)
    assert rows % tr == 0
    return pl.pallas_call(
        _bias_kernel,
        grid=(rows // tr,),
        in_specs=[pl.BlockSpec(memory_space=pltpu.SMEM),
                  pl.BlockSpec((tr, LANES), lambda i: (i, 0))],
        out_specs=pl.BlockSpec((N_HEADS, tr, LANES), lambda i: (0, i, 0)),
        out_shape=jax.ShapeDtypeStruct((N_HEADS, rows, LANES), F32),
        name="rel_bias",
    )(rel_table.astype(F32), dist)


def _group_rms_scale(z, gsum, gexp):
    hi, lo = _split_bf16(z * z)
    ssq = _dot(jnp.concatenate([hi, lo], axis=1), gsum)
    r = lax.rsqrt(ssq * (1.0 / QK_HEAD_DIM) + RMS_EPS)
    r_hi, r_lo = _split_bf16(r)
    return _dot(jnp.concatenate([r_hi, r_lo], axis=1), gexp)


def _proj_kernel(x_ref, g_ref, w_ref, qg_ref, kg_ref, gsum_ref, gexp_ref,
                 qt_ref, kf_ref, kb_ref, vf_ref, vt_ref, gb_ref, u_ref, *, aw):
    x = x_ref[...]
    ms = jnp.mean(x * x, axis=-1, keepdims=True)
    xn = (x * lax.rsqrt(ms + RMS_EPS) * g_ref[...]).astype(BF16)
    gsum = gsum_ref[...]
    gexp = gexp_ref[...]

    zq = _dot(xn, w_ref[:, 0 * aw:1 * aw])
    q = zq * _group_rms_scale(zq, gsum, gexp) * qg_ref[...]
    qt_ref[...] = (q * (QK_HEAD_DIM ** -0.5 * LOG2E)).T.astype(BF16)

    zk = _dot(xn, w_ref[:, 1 * aw:2 * aw])
    k = zk * _group_rms_scale(zk, gsum, gexp) * kg_ref[...]
    kf_ref[...] = k
    kb_ref[...] = k.astype(BF16)

    v = _dot(xn, w_ref[:, 2 * aw:3 * aw])
    vf_ref[...] = v
    vt_ref[...] = v.T.astype(BF16)

    gb_ref[...] = _dot(xn, w_ref[:, 3 * aw:4 * aw])
    u_ref[...] = _dot(xn, w_ref[:, 4 * aw:5 * aw]) * _dot(xn, w_ref[:, 5 * aw:6 * aw])


def _project(x, g, w_bf16, q_gain, k_gain, gsum, gexp):
    rows, d = x.shape
    pw = w_bf16.shape[1]
    aw = pw // 6
    tm = min(rows, 256)
    assert rows % tm == 0 and aw % LANES == 0
    row_blk = lambda width: pl.BlockSpec((tm, width), lambda i: (i, 0))
    col_blk = pl.BlockSpec((aw, tm), lambda i: (0, i))
    const = lambda shape: pl.BlockSpec(shape, lambda i: (0, 0))
    vmem = (d * pw * 2 + 2 * tm * d * 4 + 2 * tm * aw * (2 + 4 + 2 + 4 + 2 + 4 + 4)
            + 8 * tm * aw * 4 + 4 * aw * (2 * LANES + 2 * LANES))
    return pl.pallas_call(
        functools.partial(_proj_kernel, aw=aw),
        grid=(rows // tm,),
        in_specs=[row_blk(d), const((1, d)),
                  pl.BlockSpec((d, pw), lambda i: (0, 0), pipeline_mode=pl.Buffered(1)),
                  const((1, aw)), const((1, aw)),
                  const((2 * aw, LANES)), const((2 * LANES, aw))],
        out_specs=[col_blk, row_blk(aw), row_blk(aw), row_blk(aw), col_blk,
                   row_blk(aw), row_blk(aw)],
        out_shape=[jax.ShapeDtypeStruct((aw, rows), BF16),
                   jax.ShapeDtypeStruct((rows, aw), F32),
                   jax.ShapeDtypeStruct((rows, aw), BF16),
                   jax.ShapeDtypeStruct((rows, aw), F32),
                   jax.ShapeDtypeStruct((aw, rows), BF16),
                   jax.ShapeDtypeStruct((rows, aw), F32),
                   jax.ShapeDtypeStruct((rows, aw), F32)],
        compiler_params=pltpu.CompilerParams(
            dimension_semantics=("arbitrary",), vmem_limit_bytes=_vmem_limit(vmem)),
        name="in_proj",
    )(x, g, w_bf16, q_gain, k_gain, gsum, gexp)


def _lambda(lq_ref, lk_ref, lam_init):
    e = jnp.exp(jnp.sum(lq_ref[...] * lk_ref[...], axis=-1, keepdims=True))
    return e[0:1] - e[1:2] + lam_init


def _prompt_attn_kernel(qt_ref, k_ref, vt_ref, bias_ref, lq_ref, lk_ref, gs_ref,
                        o_ref, m_scr, l_scr, acc_scr, sa_scr, sb_scr, *, t, nd, lam_init):
    qi = pl.program_id(1)
    qt = qt_ref[...]
    row = lax.broadcasted_iota(jnp.int32, qt.shape, 0)
    zero = jnp.zeros_like(qt)
    qmaps = (jnp.where(row < QK_HEAD_DIM, qt, zero), jnp.where(row >= QK_HEAD_DIM, qt, zero))
    m_scr[...] = jnp.full(m_scr.shape, NEG_INF, F32)
    l_scr[...] = jnp.zeros(l_scr.shape, F32)
    acc_scr[...] = jnp.zeros(acc_scr.shape, F32)

    def scores(ki, s_scr):
        kb = k_ref[pl.ds(pl.multiple_of(ki * t, t), t), :]
        for c in range(2):
            s_scr[c] = _dot(kb, qmaps[c])

    def attend(ki, s_scr, delta):
        vt = vt_ref[:, pl.ds(pl.multiple_of(ki * t, t), t)]
        b = bias_ref[0, delta]
        for c in range(2):
            s = s_scr[c] + b
            m_prev = m_scr[c]
            m_new = jnp.maximum(m_prev, jnp.max(s, axis=0, keepdims=True))
            p = jnp.exp2(s - m_new)
            alpha = jnp.exp2(m_prev - m_new)
            l_scr[c] = alpha * l_scr[c] + jnp.sum(p, axis=0, keepdims=True)
            acc_scr[c] = alpha * acc_scr[c] + _dot(vt, p.astype(BF16))
            m_scr[c] = m_new

    scores(0, sa_scr)

    def body(j, carry):
        k0 = 2 * j
        scores(k0 + 1, sb_scr)
        attend(k0, sa_scr, jnp.minimum(qi - k0, nd))
        scores(k0 + 2, sa_scr)
        attend(k0 + 1, sb_scr, jnp.minimum(qi - k0 - 1, nd))
        return carry

    lax.fori_loop(0, qi // 2, body, 0)

    @pl.when(qi % 2 == 0)
    def _():
        attend(qi, sa_scr, 0)

    @pl.when(qi % 2 == 1)
    def _():
        scores(qi, sb_scr)
        attend(qi - 1, sa_scr, min(1, nd))
        attend(qi, sb_scr, 0)

    lam = _lambda(lq_ref, lk_ref, lam_init)
    o = acc_scr[0] / l_scr[0] - lam * (acc_scr[1] / l_scr[1])
    o = o * lax.rsqrt(jnp.mean(o * o, axis=0, keepdims=True) + RMS_EPS)
    o_ref[...] = (o.T * gs_ref[...] * (1.0 - lam_init)).astype(o_ref.dtype)


def _prompt_attention(qt, kb, vt, bias, lq, lk, g_sub, *, t, nd, lam_init):
    aw, s = qt.shape
    vmem = (2 * t * LANES * 2 + 2 * LANES * s * 2 + 2 * s * LANES * 2
            + 2 * (nd + 1) * t * t * 4 + 2 * t * LANES * 2
            + 2 * t * (2 * SUBLANES + LANES) * 4 + 4 * t * t * 4 + 8 * t * t * 4)
    return pl.pallas_call(
        functools.partial(_prompt_attn_kernel, t=t, nd=nd, lam_init=lam_init),
        grid=(N_HEADS, s // t),
        in_specs=[pl.BlockSpec((V_HEAD_DIM, t), lambda h, i: (h, i)),
                  pl.BlockSpec((s, V_HEAD_DIM), lambda h, i: (0, h)),
                  pl.BlockSpec((V_HEAD_DIM, s), lambda h, i: (h, 0)),
                  pl.BlockSpec((1, nd + 1, t, t), lambda h, i: (h, 0, 0, 0)),
                  pl.BlockSpec((2, QK_HEAD_DIM), lambda h, i: (0, 0)),
                  pl.BlockSpec((2, QK_HEAD_DIM), lambda h, i: (0, 0)),
                  pl.BlockSpec((1, V_HEAD_DIM), lambda h, i: (0, 0))],
        out_specs=pl.BlockSpec((t, V_HEAD_DIM), lambda h, i: (i, h)),
        out_shape=jax.ShapeDtypeStruct((s, aw), BF16),
        scratch_shapes=[pltpu.VMEM((2, 1, t), F32), pltpu.VMEM((2, 1, t), F32),
                        pltpu.VMEM((2, V_HEAD_DIM, t), F32),
                        pltpu.VMEM((2, t, t), F32), pltpu.VMEM((2, t, t), F32)],
        compiler_params=pltpu.CompilerParams(
            dimension_semantics=("arbitrary", "arbitrary"),
            vmem_limit_bytes=_vmem_limit(vmem)),
        name="prompt_attn",
    )(qt, kb, vt, bias, lq, lk, g_sub)


def _sample_attn_kernel(pt_ref, qr_ref, kn_ref, vn_ref, bias_ref, bself_ref, lq_ref, lk_ref,
                        gs_ref, *rest, pages, page, lam_init):
    del pt_ref
    k_refs = rest[:pages]
    v_refs = rest[pages:2 * pages]
    o_ref, m_scr, l_scr, acc_scr = rest[2 * pages:]
    step = pl.program_id(1)
    qr = qr_ref[0]
    half = qr.shape[0] // 2

    def both_maps(b):
        return jnp.concatenate([b, b], axis=0)

    def update(s, vs):
        m_prev = m_scr[...]
        m_new = jnp.maximum(m_prev, jnp.max(s, axis=-1, keepdims=True))
        p = jnp.exp2(s - m_new)
        alpha = jnp.exp2(m_prev - m_new)
        l_scr[...] = alpha * l_scr[...] + jnp.sum(p, axis=-1, keepdims=True)
        p = p.astype(BF16)
        acc = alpha * acc_scr[...]
        for j, vj in enumerate(vs):
            acc = acc + _dot(p[:, j * page:(j + 1) * page], vj)
        acc_scr[...] = acc
        m_scr[...] = m_new

    @pl.when(step == 0)
    def _():
        m_scr[...] = jnp.full(m_scr.shape, NEG_INF, F32)
        l_scr[...] = jnp.zeros(l_scr.shape, F32)
        acc_scr[...] = jnp.zeros(acc_scr.shape, F32)
        update(_dot(qr, kn_ref[0]) + both_maps(bself_ref[...]), [vn_ref[0]])

    def v_page(ref):
        return jnp.concatenate(
            [ref[0, pl.ds(h, page, stride=N_HEADS), :] for h in range(N_HEADS)],
            axis=1).astype(BF16)

    s = jnp.concatenate(
        [_dot(qr, k_refs[j][0].astype(BF16)) + both_maps(bias_ref[j]) for j in range(pages)],
        axis=1)
    update(s, [v_page(v_refs[j]) for j in range(pages)])

    @pl.when(step == pl.num_programs(1) - 1)
    def _():
        o = acc_scr[...] / l_scr[...]
        lam = _lambda(lq_ref, lk_ref, lam_init)
        o = o[:half] - lam * o[half:]
        row_head = lax.broadcasted_iota(jnp.int32, o.shape, 0) % N_HEADS
        lane_head = lax.broadcasted_iota(jnp.int32, o.shape, 1) // V_HEAD_DIM
        o = jnp.where(row_head == lane_head, o, 0.0)
        ms = jnp.sum(o * o, axis=-1, keepdims=True) * (1.0 / V_HEAD_DIM)
        o = o * lax.rsqrt(ms + RMS_EPS)
        o = jnp.sum(o.reshape(half // N_HEADS, N_HEADS, o.shape[1]), axis=1)
        o_ref[0] = (o * gs_ref[...] * (1.0 - lam_init)).astype(o_ref.dtype)


def _sample_attention(page_table, q_rows, kt_new, v_new, bias_pages, bias_self, lq, lk, g_sub_t,
                      cache_kt, cache_vr, *, pages, lam_init):
    nb, n_pages = page_table.shape
    _, aw, page = cache_kt.shape
    dt = q_rows.shape[1] // (2 * N_HEADS)
    steps = n_pages // pages
    assert n_pages % pages == 0
    rows = q_rows.shape[1]

    def page_spec(shape):
        return [pl.BlockSpec((1,) + shape,
                             lambda b, s, pt, j=j: (pt[b * n_pages + s * pages + j], 0, 0))
                for j in range(pages)]

    per_b = lambda shape: pl.BlockSpec(shape, lambda b, s, pt: (b, 0, 0))
    const2 = lambda shape: pl.BlockSpec(shape, lambda b, s, pt: (0, 0))
    vmem = (2 * 2 * pages * page * aw * 4 + 2 * pages * page * aw * 2
            + 4 * rows * aw * 4 + 2 * 3 * rows * aw * 2)
    grid_spec = pltpu.PrefetchScalarGridSpec(
        num_scalar_prefetch=1,
        grid=(nb, steps),
        in_specs=[per_b((1, rows, aw)), per_b((1, aw, page)), per_b((1, page, aw)),
                  pl.BlockSpec((pages, rows // 2, page), lambda b, s, pt: (s, 0, 0)),
                  const2((rows // 2, page)),
                  const2((2, QK_HEAD_DIM)), const2((2, QK_HEAD_DIM)), const2((1, aw))]
                 + page_spec((aw, page)) + page_spec((page * N_HEADS, V_HEAD_DIM)),
        out_specs=pl.BlockSpec((1, dt, aw), lambda b, s, pt: (b, 0, 0)),
        scratch_shapes=[pltpu.VMEM((rows, 1), F32), pltpu.VMEM((rows, 1), F32),
                        pltpu.VMEM((rows, aw), F32)])
    return pl.pallas_call(
        functools.partial(_sample_attn_kernel, pages=pages, page=page, lam_init=lam_init),
        grid_spec=grid_spec,
        out_shape=jax.ShapeDtypeStruct((nb, dt, aw), BF16),
        compiler_params=pltpu.CompilerParams(
            dimension_semantics=("arbitrary", "arbitrary"),
            vmem_limit_bytes=_vmem_limit(vmem)),
        name="sample_attn",
    )(page_table.reshape(-1), q_rows, kt_new, v_new, bias_pages, bias_self, lq, lk, g_sub_t,
      *([cache_kt] * pages), *([cache_vr] * pages))


def _out_proj_kernel(*refs, halo):
    if halo:
        a_ref, gb_ref, u_ref, prev_ref, cw_ref, wo_ref, x_ref, o_ref = refs
        u = u_ref[...]
        first = pl.program_id(0) == 0
        prev = jnp.where(first, 0.0, prev_ref[...])
        ucat = jnp.concatenate([prev, u], axis=0)
        p1 = pltpu.roll(ucat, 1, axis=0)[SUBLANES:]
        p2 = pltpu.roll(ucat, 2, axis=0)[SUBLANES:]
    else:
        a_ref, gb_ref, u_ref, p1_ref, p2_ref, cw_ref, wo_ref, x_ref, o_ref = refs
        u, p1, p2 = u_ref[...], p1_ref[...], p2_ref[...]
    cw = cw_ref[...]
    aw = u.shape[1]
    c = gb_ref[...] * (p2 * cw[0:1] + p1 * cw[1:2] + u * cw[2:3])
    y = _dot(a_ref[...], wo_ref[:aw, :]) + _dot(c.astype(BF16), wo_ref[aw:, :])
    o_ref[...] = x_ref[...] + y


def _out_project(a, gb, u, conv_w, wo_bf16, x, prev=None):
    rows, d = x.shape
    aw = a.shape[1]
    tm = min(rows, 256)
    assert rows % tm == 0
    row_blk = lambda width: pl.BlockSpec((tm, width), lambda i: (i, 0))
    const = lambda shape: pl.BlockSpec(shape, lambda i: (0, 0))
    halo = prev is None
    if halo:
        nsub = tm // SUBLANES
        extra = [pl.BlockSpec((SUBLANES, aw), lambda i: (jnp.maximum(i * nsub - 1, 0), 0))]
        extra_args = [u]
    else:
        extra = [row_blk(aw), row_blk(aw)]
        extra_args = list(prev)
    vmem = 2 * d * d * 2 + 2 * tm * (aw * (2 + 4 + 4 + 8) + d * 8) + 6 * tm * d * 4
    return pl.pallas_call(
        functools.partial(_out_proj_kernel, halo=halo),
        grid=(rows // tm,),
        in_specs=[row_blk(aw), row_blk(aw), row_blk(aw)] + extra
                 + [const((CONV_K, aw)), const((d, d)), row_blk(d)],
        out_specs=row_blk(d),
        out_shape=jax.ShapeDtypeStruct((rows, d), F32),
        compiler_params=pltpu.CompilerParams(
            dimension_semantics=("arbitrary",), vmem_limit_bytes=_vmem_limit(vmem)),
        name="out_proj",
    )(a, gb, u, *extra_args, conv_w, wo_bf16, x)


ROUTER_ROWS = 32
EXPERT_ROW0 = 8


def _router_kernel(x_ref, g_ref, w3_ref, bias_ref, idx_ref, wt_ref):
    x = x_ref[...]
    ms = jnp.mean(x * x, axis=-1, keepdims=True)
    xn = x * lax.rsqrt(ms + RMS_EPS) * g_ref[...]
    hi, lo = _split_bf16(xn)
    logits = _dot_nt(w3_ref[...], jnp.concatenate([hi, hi, lo], axis=1)) + bias_ref[...]
    big = jnp.int32(1 << 20)

    gl = logits[0:SUBLANES]
    grow = lax.broadcasted_iota(jnp.int32, gl.shape, 0)
    gmax = jnp.max(gl, axis=0, keepdims=True)
    g_idx = jnp.min(jnp.where(gl == gmax, grow, big), axis=0, keepdims=True)
    g_w = 1.0 / jnp.sum(jnp.exp(gl - gmax), axis=0, keepdims=True)

    el = logits[EXPERT_ROW0:EXPERT_ROW0 + N_EXPERTS]
    erow = lax.broadcasted_iota(jnp.int32, el.shape, 0)
    el = jnp.where(erow // EXPERTS_PER_GROUP == g_idx, el, NEG_INF)
    v1 = jnp.max(el, axis=0, keepdims=True)
    i1 = jnp.min(jnp.where(el == v1, erow, big), axis=0, keepdims=True)
    el2 = jnp.where(erow == i1, NEG_INF, el)
    v2 = jnp.max(el2, axis=0, keepdims=True)
    i2 = jnp.min(jnp.where(el2 == v2, erow, big), axis=0, keepdims=True)
    t = jnp.exp(v2 - v1)
    w1 = g_w / (1.0 + t)
    idx_ref[...] = jnp.concatenate([i1, i2], axis=0)
    wt_ref[...] = jnp.concatenate([w1, w1 * t], axis=0)


def _route(x, g, w3, bias_col):
    rows, d = x.shape
    tm = math.gcd(rows, 256)
    assert tm % LANES == 0
    return pl.pallas_call(
        _router_kernel,
        grid=(rows // tm,),
        in_specs=[pl.BlockSpec((tm, d), lambda i: (i, 0)),
                  pl.BlockSpec((1, d), lambda i: (0, 0)),
                  pl.BlockSpec((ROUTER_ROWS, 3 * d), lambda i: (0, 0)),
                  pl.BlockSpec((ROUTER_ROWS, 1), lambda i: (0, 0))],
        out_specs=[pl.BlockSpec((2, tm), lambda i: (0, i)),
                   pl.BlockSpec((2, tm), lambda i: (0, i))],
        out_shape=[jax.ShapeDtypeStruct((2, rows), jnp.int32),
                   jax.ShapeDtypeStruct((2, rows), F32)],
        compiler_params=pltpu.CompilerParams(dimension_semantics=("arbitrary",)),
        name="router",
    )(x, g, w3, bias_col)


EXPERT_PHASES = 4


def _expert_kernel(te_ref, src_cur, src_next, dst_prev, dst_cur, roww_ref, g_ref,
                   wg_ref, wu_ref, wd_ref, x_hbm, y_hbm,
                   xbuf, ybuf, wgb, wub, wdb, gsem, ssem, *, tm):
    j = pl.program_id(0)
    last = pl.num_programs(0) - 1
    slot = j % 2
    other = 1 - slot

    def gather_rows(idx_ref, buf, lo, hi):
        for r in range(lo, hi):
            pltpu.make_async_copy(x_hbm.at[pl.ds(idx_ref[0, 0, r], 1)],
                                  xbuf.at[buf, pl.ds(r, 1)], gsem.at[buf]).start()

    def scatter_rows(idx_ref, buf, lo, hi):
        for r in range(lo, hi):
            pltpu.make_async_copy(ybuf.at[buf, pl.ds(r, 1)],
                                  y_hbm.at[pl.ds(idx_ref[0, 0, r], 1)], ssem.at[buf]).start()

    def wait_gather(buf):
        pltpu.make_async_copy(x_hbm.at[pl.ds(0, tm)], xbuf.at[buf], gsem.at[buf]).wait()

    def wait_scatter(buf):
        pltpu.make_async_copy(ybuf.at[buf], y_hbm.at[pl.ds(0, tm)], ssem.at[buf]).wait()

    @pl.when(j == 0)
    def _():
        ybuf[1] = jnp.zeros(ybuf.shape[1:], F32)
        gather_rows(src_cur, 0, 0, tm)

    @pl.when(jnp.logical_or(j == 0, te_ref[j] != te_ref[jnp.maximum(j - 1, 0)]))
    def _():
        wgb[...] = wg_ref[0].astype(BF16)
        wub[...] = wu_ref[0].astype(BF16)
        wdb[...] = wd_ref[0].astype(BF16)

    wait_gather(slot)

    @pl.when(j >= 1)
    def _():
        wait_scatter(slot)

    x = xbuf[slot]
    ms = jnp.mean(x * x, axis=-1, keepdims=True)
    xn = (x * lax.rsqrt(ms + RMS_EPS) * g_ref[...]).astype(BF16)
    roww = roww_ref[:, 0:1]
    resid = roww_ref[:, 1:2]
    ff = wgb.shape[1]
    d = wdb.shape[1]
    step = tm // EXPERT_PHASES
    half_ff, half_d = ff // 2, d // 2
    hs = []
    for c in range(2):
        gather_rows(src_next, other, (2 * c) * step, (2 * c + 2) * step)
        cols = slice(c * half_ff, (c + 1) * half_ff)
        hg = _dot(xn, wgb[:, cols])
        hu = _dot(xn, wub[:, cols])
        hs.append((hg * jax.nn.sigmoid(hg) * hu * roww).astype(BF16))
    h = jnp.concatenate(hs, axis=1)
    for c in range(2):
        scatter_rows(dst_prev, other, (2 * c) * step, (2 * c + 2) * step)
        cols = slice(c * half_d, (c + 1) * half_d)
        ybuf[slot, :, cols] = _dot(h, wdb[:, cols]) + x[:, cols] * resid

    @pl.when(j == last)
    def _():
        scatter_rows(dst_cur, slot, 0, tm)
        wait_gather(other)
        wait_scatter(other)
        wait_scatter(slot)


def _experts(tile_expert, src_cur, src_next, dst_prev, dst_cur, row_w, g, w_gate, w_up, w_down,
             x, *, tm, out_rows):
    nt = tile_expert.shape[0]
    d = x.shape[1]
    ff = w_gate.shape[2]
    assert tm % EXPERT_PHASES == 0
    smem_row = pl.BlockSpec((1, 1, tm), lambda i, te: (i, 0, 0), memory_space=pltpu.SMEM)
    expert_w = lambda shape: pl.BlockSpec((1,) + shape, lambda i, te: (te[i], 0, 0))
    grid_spec = pltpu.PrefetchScalarGridSpec(
        num_scalar_prefetch=1,
        grid=(nt,),
        in_specs=[smem_row, smem_row, smem_row, smem_row,
                  pl.BlockSpec((tm, 2), lambda i, te: (i, 0)),
                  pl.BlockSpec((1, d), lambda i, te: (0, 0)),
                  expert_w((d, ff)), expert_w((d, ff)), expert_w((ff, d)),
                  pl.BlockSpec(memory_space=pl.ANY)],
        out_specs=pl.BlockSpec(memory_space=pl.ANY),
        scratch_shapes=[pltpu.VMEM((2, tm, d), F32), pltpu.VMEM((2, tm, d), F32),
                        pltpu.VMEM((d, ff), BF16), pltpu.VMEM((d, ff), BF16),
                        pltpu.VMEM((ff, d), BF16),
                        pltpu.SemaphoreType.DMA((2,)), pltpu.SemaphoreType.DMA((2,))])
    vmem = 2 * 3 * d * ff * 4 + 3 * d * ff * 2 + 4 * tm * d * 4 + 6 * tm * d * 4
    return pl.pallas_call(
        functools.partial(_expert_kernel, tm=tm),
        grid_spec=grid_spec,
        out_shape=jax.ShapeDtypeStruct((out_rows, d), F32),
        compiler_params=pltpu.CompilerParams(
            dimension_semantics=("arbitrary",), vmem_limit_bytes=_vmem_limit(vmem)),
        name="experts",
    )(tile_expert, src_cur, src_next, dst_prev, dst_cur, row_w, g, w_gate, w_up, w_down, x)


def _combine_kernel(y0_ref, y1_ref, o_ref):
    o_ref[...] = y0_ref[...] + y1_ref[...]


def _combine(y2, row0, rows, total_rows):
    d = y2.shape[1]
    tm = math.gcd(math.gcd(rows, row0) if row0 else rows, LANES)
    assert total_rows % tm == 0 and tm % SUBLANES == 0
    b0 = row0 // tm
    b1 = (total_rows + row0) // tm
    return pl.pallas_call(
        _combine_kernel,
        grid=(rows // tm,),
        in_specs=[pl.BlockSpec((tm, d), lambda i: (b0 + i, 0)),
                  pl.BlockSpec((tm, d), lambda i: (b1 + i, 0))],
        out_specs=pl.BlockSpec((tm, d), lambda i: (i, 0)),
        out_shape=jax.ShapeDtypeStruct((rows, d), F32),
        compiler_params=pltpu.CompilerParams(dimension_semantics=("arbitrary",)),
        name="moe_combine",
    )(y2, y2)


def _routing_plan(e_idx, w, tm):
    total = e_idx.shape[1]
    n_assign = 2 * total
    nt = -(-n_assign // tm) + N_EXPERTS
    e_flat = e_idx.reshape(-1)
    w_flat = w.reshape(-1)
    order = jnp.argsort(e_flat, stable=True).astype(jnp.int32)
    counts = jnp.sum((e_flat[:, None] == jnp.arange(N_EXPERTS, dtype=jnp.int32)[None, :])
                     .astype(jnp.int32), axis=0)
    tiles_per = (counts + tm - 1) // tm
    tile_end = jnp.cumsum(tiles_per)
    tile_start = tile_end - tiles_per
    group_start = jnp.cumsum(counts) - counts
    tiles = jnp.arange(nt, dtype=jnp.int32)
    tile_expert = jnp.minimum(
        jnp.sum((tiles[:, None] >= tile_end[None, :]).astype(jnp.int32), axis=1), N_EXPERTS - 1)
    tile_valid = (tiles < tile_end[-1]).astype(jnp.int32)
    rank = (tiles - tile_start[tile_expert])[:, None] * tm + jnp.arange(tm, dtype=jnp.int32)[None, :]
    valid = (rank < counts[tile_expert][:, None]) & (tile_valid[:, None] > 0)
    pos = jnp.clip(group_start[tile_expert][:, None] + rank, 0, n_assign - 1)
    assign = order[pos]
    pad_rank = jnp.cumsum(jnp.logical_not(valid).reshape(-1).astype(jnp.int32)) - 1
    dump = n_assign + pad_rank.reshape(nt, tm)
    row_src = jnp.where(valid, assign % total, 0)
    row_dst = jnp.where(valid, assign, dump)
    row_w = jnp.stack([jnp.where(valid, w_flat[assign], 0.0),
                       (valid & (assign < total)).astype(F32)], axis=-1)
    src_next = jnp.concatenate([row_src[1:], row_src[-1:]], axis=0)
    first_dst = nt * tm + jnp.arange(tm, dtype=jnp.int32)[None, :]
    dst_prev = jnp.concatenate([first_dst, row_dst[:-1]], axis=0)
    as_smem = lambda a: a.reshape(nt, 1, tm)
    return (tile_expert, as_smem(row_src), as_smem(src_next), as_smem(dst_prev), as_smem(row_dst),
            row_w.reshape(nt * tm, 2)), (nt + 1) * tm


def _group_matrices(aw):
    e = jnp.arange(aw, dtype=jnp.int32) // QK_HEAD_DIM
    lanes = jnp.arange(LANES, dtype=jnp.int32)
    member = (e[:, None] == lanes[None, :]).astype(BF16)
    return jnp.concatenate([member, member], axis=0), jnp.concatenate([member.T, member.T], axis=0)


def _toeplitz_kernel(strip_ref, out_ref, *, t):
    x = jnp.broadcast_to(strip_ref[0], (t, 2 * t))
    out_ref[0, 0] = pltpu.roll(x, t + 1, axis=1, stride=1, stride_axis=0)[:, :t]


def _prompt_bias(rel_table, t, nd):
    m = jnp.arange(2 * t, dtype=jnp.int32)
    dist = jnp.arange(nd + 1, dtype=jnp.int32)[:, None] * t - (t - 1) + m[None, :]
    strips = _bias_table(rel_table, dist.reshape(-1, LANES))
    strips = strips.reshape(N_HEADS * (nd + 1), 1, 2 * t)
    return pl.pallas_call(
        functools.partial(_toeplitz_kernel, t=t),
        grid=(N_HEADS, nd + 1),
        in_specs=[pl.BlockSpec((1, 1, 2 * t), lambda h, dl: (h * (nd + 1) + dl, 0, 0))],
        out_specs=pl.BlockSpec((1, 1, t, t), lambda h, dl: (h, dl, 0, 0)),
        out_shape=jax.ShapeDtypeStruct((N_HEADS, nd + 1, t, t), F32),
        name="bias_toeplitz",
    )(strips)


def _sample_bias(rel_table, past, n_pages, page, dt):
    tq = jnp.arange(dt, dtype=jnp.int32)
    kpos = jnp.arange(n_pages * page, dtype=jnp.int32).reshape(n_pages, 1, page)
    dist_pages = (past + tq[None, :, None] - kpos).reshape(-1, page)
    j = jnp.arange(page, dtype=jnp.int32)
    self_rows = jnp.where(j[None, :] < dt, tq[:, None] - j[None, :], -1)
    dist = jnp.concatenate([dist_pages, self_rows], axis=0)
    rows = dist.shape[0]
    rpad = (-rows) % (256 if rows > 256 else SUBLANES)
    dist = jnp.concatenate([dist, jnp.full((rpad, page), -1, jnp.int32)], axis=0)
    bias = _bias_table(rel_table, dist)
    bp = bias[:, :n_pages * dt].reshape(N_HEADS, n_pages, dt, page)
    bp = bp.transpose(1, 2, 0, 3).reshape(n_pages, dt * N_HEADS, page)
    bs = bias[:, n_pages * dt:n_pages * dt + dt].transpose(1, 0, 2).reshape(dt * N_HEADS, page)
    return bp, bs


def kernel(x_prompt, x_sample, cache_k, cache_v, state_conv, page_table, rel_table, attn_norm_g,
           w_in, q_norm_g, k_norm_g, lambda_q, lambda_k, sub_norm_g, conv_w, w_o, ffn_norm_g,
           router_group_w, router_group_b, router_expert_w, router_expert_b, w_gate, w_up, w_down):
    depth = w_in.shape[0]
    assert depth == 1, "single-layer trunk"
    batch, seq, d = x_prompt.shape
    assert batch == 1, "one prompt sequence"
    nb, dt, _ = x_sample.shape
    n_pool, page = cache_k.shape[1], cache_k.shape[2]
    n_pages = page_table.shape[1]
    past = n_pages * page
    aw = N_HEADS * V_HEAD_DIM
    layer = 0
    lam_init = 0.8 - 0.6 * math.exp(-0.3 * layer)

    w_in_b = w_in[layer].astype(BF16)
    w_o_b = w_o[layer].astype(BF16)
    g_attn = attn_norm_g[layer].reshape(1, d)
    g_ffn = ffn_norm_g[layer].reshape(1, d)
    qg = jnp.tile(q_norm_g[layer], aw // QK_HEAD_DIM).reshape(1, aw)
    kg = jnp.tile(k_norm_g[layer], aw // QK_HEAD_DIM).reshape(1, aw)
    g_sub = sub_norm_g[layer].reshape(1, V_HEAD_DIM)
    g_sub_t = jnp.tile(sub_norm_g[layer], N_HEADS).reshape(1, aw)
    gsum, gexp = _group_matrices(aw)
    lq, lk = lambda_q[layer], lambda_k[layer]
    cw = conv_w[layer]

    xp = x_prompt.reshape(seq, d)
    xs = x_sample.reshape(nb * dt, d)

    qtp, kp, kbp, vp, vtp, gbp, up = _project(xp, g_attn, w_in_b, qg, kg, gsum, gexp)
    qts, ks, _, vs, _, gbs, us = _project(xs, g_attn, w_in_b, qg, kg, gsum, gexp)

    t = min(seq, 512)
    assert seq % t == 0
    nd = min(-(-(MAX_DISTANCE + t - 1) // t), seq // t)
    bias_p = _prompt_bias(rel_table, t, nd)
    ap = _prompt_attention(qtp, kbp, vtp, bias_p, lq, lk, g_sub, t=t, nd=nd, lam_init=lam_init)

    bias_pages, bias_self = _sample_bias(rel_table, past, n_pages, page, dt)
    lane_hc = jnp.arange(aw, dtype=jnp.int32) // QK_HEAD_DIM
    row_c = jnp.arange(2, dtype=jnp.int32)[:, None, None]
    row_h = jnp.arange(N_HEADS, dtype=jnp.int32)[None, None, :]
    sel = (lane_hc[None, None, None, :] == (row_h * 2 + row_c)[..., None])
    q3 = qts.T.reshape(nb, 1, dt, 1, aw)
    q_rows = jnp.where(sel[None], q3, jnp.zeros((), BF16)).reshape(nb, 2 * dt * N_HEADS, aw)
    k3 = ks.reshape(nb, dt, aw).astype(BF16)
    kt_new = jnp.pad(k3.transpose(0, 2, 1), ((0, 0), (0, 0), (0, page - dt)))
    v_new = jnp.pad(vs.reshape(nb, dt, aw).astype(BF16), ((0, 0), (0, page - dt), (0, 0)))
    cache_kt = cache_k[layer].transpose(0, 2, 3, 4, 1).reshape(n_pool, aw, page)
    cache_vr = cache_v[layer].reshape(n_pool, page * N_HEADS, V_HEAD_DIM)
    pages_per_step = math.gcd(n_pages, PAGES_PER_STEP)
    a_s = _sample_attention(page_table, q_rows, kt_new, v_new, bias_pages, bias_self,
                            lq, lk, g_sub_t, cache_kt, cache_vr,
                            pages=pages_per_step, lam_init=lam_init)
    a_s = a_s.reshape(nb * dt, aw)

    x2p = _out_project(ap, gbp, up, cw, w_o_b, xp)
    u_ext = jnp.concatenate([state_conv[layer].astype(F32), us.reshape(nb, dt, aw)], axis=1)
    prev1 = u_ext[:, 1:1 + dt].reshape(nb * dt, aw)
    prev2 = u_ext[:, 0:dt].reshape(nb * dt, aw)
    x2s = _out_project(a_s, gbs, us, cw, w_o_b, xs, prev=(prev1, prev2))
    x2 = jnp.concatenate([x2p, x2s], axis=0)
    total = seq + nb * dt

    wr = jnp.zeros((ROUTER_ROWS, d), F32)
    wr = wr.at[0:N_GROUPS].set(router_group_w[layer].T)
    wr = wr.at[EXPERT_ROW0:EXPERT_ROW0 + N_EXPERTS].set(router_expert_w[layer].T)
    wr_hi, wr_lo = _split_bf16(wr)
    w3 = jnp.concatenate([wr_hi, wr_lo, wr_hi], axis=1)
    bias_col = jnp.full((ROUTER_ROWS,), NEG_INF, F32)
    bias_col = bias_col.at[0:N_GROUPS].set(router_group_b[layer].astype(F32))
    bias_col = bias_col.at[EXPERT_ROW0:EXPERT_ROW0 + N_EXPERTS].set(
        router_expert_b[layer].reshape(-1).astype(F32)).reshape(ROUTER_ROWS, 1)
    e_idx, wts = _route(x2, g_ffn, w3, bias_col)
    tm_e = 256
    plan, y_rows = _routing_plan(e_idx, wts, tm_e)
    y2 = _experts(*plan, g_ffn, w_gate[layer], w_up[layer], w_down[layer], x2,
                  tm=tm_e, out_rows=y_rows)
    yp = _combine(y2, 0, seq, total)
    ys = _combine(y2, seq, nb * dt, total)

    return (yp.reshape(batch, seq, d),
            ys.reshape(nb, dt, d),
            kp.reshape(depth, batch, seq, N_HEADS, 2, QK_HEAD_DIM),
            vp.reshape(depth, batch, seq, N_HEADS, V_HEAD_DIM),
            up[seq - (CONV_K - 1):].reshape(depth, batch, CONV_K - 1, aw),
            ks.reshape(depth, nb, dt, N_HEADS, 2, QK_HEAD_DIM),
            vs.reshape(depth, nb, dt, N_HEADS, V_HEAD_DIM),
            u_ext[:, dt:].reshape(depth, nb, CONV_K - 1, aw))
```

```python
import functools
import math

import jax
import jax.numpy as jnp
from jax import lax
from jax.experimental import pallas as pl
from jax.experimental.pallas import tpu as pltpu

N_HEADS = 8
V_HEAD_DIM = 128
QK_HEAD_DIM = 64
CONV_K = 3
N_BUCKETS = 32
MAX_EXACT = N_BUCKETS // 2
MAX_DISTANCE = 2048
N_GROUPS = 4
EXPERTS_PER_GROUP = 4
N_EXPERTS = N_GROUPS * EXPERTS_PER_GROUP
RMS_EPS = 1e-6
NEG_INF = -1e30
LOG2E = math.log2(math.e)

LANES = 128
SUBLANES = 8
VMEM_PHYSICAL_BYTES = 64 * 1024 * 1024
PAGES_PER_STEP = 16

F32 = jnp.float32
BF16 = jnp.bfloat16


def _vmem_limit(nbytes):
    return int(min(max(nbytes * 5 // 4 + (4 << 20), 16 << 20), VMEM_PHYSICAL_BYTES - (6 << 20)))


def _dot(a, b):
    return jnp.dot(a, b, preferred_element_type=F32)


def _dot_nt(a, b):
    return lax.dot_general(a, b, (((1,), (1,)), ((), ())), preferred_element_type=F32)


def _split_bf16(x):
    hi = x.astype(BF16)
    lo = (x - hi.astype(F32)).astype(BF16)
    return hi, lo


def _bias_kernel(table_ref, dist_ref, out_ref):
    d = dist_ref[...]
    n = jnp.maximum(d, 0)
    nf = jnp.maximum(n, 1).astype(F32)
    large = MAX_EXACT + (jnp.log(nf / MAX_EXACT) / math.log(MAX_DISTANCE / MAX_EXACT)
                         * (N_BUCKETS - MAX_EXACT)).astype(jnp.int32)
    large = jnp.minimum(large, N_BUCKETS - 1)
    bucket = jnp.where(n < MAX_EXACT, n, large)
    hits = [bucket == b for b in range(N_BUCKETS)]
    for h in range(N_HEADS):
        acc = jnp.zeros(d.shape, F32)
        for b in range(N_BUCKETS):
            acc = jnp.where(hits[b], table_ref[b, h] * LOG2E, acc)
        out_ref[h] = jnp.where(d < 0, NEG_INF, acc)


def _bias_table(rel_table, dist):
    rows = dist.shape[0]
    tr = min(rows, 256)
    assert rows % tr == 0
    return pl.pallas_call(
        _bias_kernel,
        grid=(rows // tr,),
        in_specs=[pl.BlockSpec(memory_space=pltpu.SMEM),
                  pl.BlockSpec((tr, LANES), lambda i: (i, 0))],
        out_specs=pl.BlockSpec((N_HEADS, tr, LANES), lambda i: (0, i, 0)),
        out_shape=jax.ShapeDtypeStruct((N_HEADS, rows, LANES), F32),
        name="rel_bias",
    )(rel_table.astype(F32), dist)


def _group_rms_scale(z, gsum, gexp):
    hi, lo = _split_bf16(z * z)
    ssq = _dot(jnp.concatenate([hi, lo], axis=1), gsum)
    r = lax.rsqrt(ssq * (1.0 / QK_HEAD_DIM) + RMS_EPS)
    r_hi, r_lo = _split_bf16(r)
    return _dot(jnp.concatenate([r_hi, r_lo], axis=1), gexp)


def _proj_kernel(x_ref, g_ref, w_ref, qg_ref, kg_ref, gsum_ref, gexp_ref,
                 qt_ref, kf_ref, kb_ref, vf_ref, vt_ref, gb_ref, u_ref, *, aw):
    x = x_ref[...]
    ms = jnp.mean(x * x, axis=-1, keepdims=True)
    xn = (x * lax.rsqrt(ms + RMS_EPS) * g_ref[...]).astype(BF16)
    gsum = gsum_ref[...]
    gexp = gexp_ref[...]

    zq = _dot(xn, w_ref[:, 0 * aw:1 * aw])
    q = zq * _group_rms_scale(zq, gsum, gexp) * qg_ref[...]
    qt_ref[...] = (q * (QK_HEAD_DIM ** -0.5 * LOG2E)).T.astype(BF16)

    zk = _dot(xn, w_ref[:, 1 * aw:2 * aw])
    k = zk * _group_rms_scale(zk, gsum, gexp) * kg_ref[...]
    kf_ref[...] = k
    kb_ref[...] = k.astype(BF16)

    v = _dot(xn, w_ref[:, 2 * aw:3 * aw])
    vf_ref[...] = v
    vt_ref[...] = v.T.astype(BF16)

    gb_ref[...] = _dot(xn, w_ref[:, 3 * aw:4 * aw])
    u_ref[...] = _dot(xn, w_ref[:, 4 * aw:5 * aw]) * _dot(xn, w_ref[:, 5 * aw:6 * aw])


def _project(x, g, w_bf16, q_gain, k_gain, gsum, gexp):
    rows, d = x.shape
    pw = w_bf16.shape[1]
    aw = pw // 6
    tm = min(rows, 256)
    assert rows % tm == 0 and aw % LANES == 0
    row_blk = lambda width: pl.BlockSpec((tm, width), lambda i: (i, 0))
    col_blk = pl.BlockSpec((aw, tm), lambda i: (0, i))
    const = lambda shape: pl.BlockSpec(shape, lambda i: (0, 0))
    vmem = (d * pw * 2 + 2 * tm * d * 4 + 2 * tm * aw * (2 + 4 + 2 + 4 + 2 + 4 + 4)
            + 8 * tm * aw * 4 + 4 * aw * (2 * LANES + 2 * LANES))
    return pl.pallas_call(
        functools.partial(_proj_kernel, aw=aw),
        grid=(rows // tm,),
        in_specs=[row_blk(d), const((1, d)),
                  pl.BlockSpec((d, pw), lambda i: (0, 0), pipeline_mode=pl.Buffered(1)),
                  const((1, aw)), const((1, aw)),
                  const((2 * aw, LANES)), const((2 * LANES, aw))],
        out_specs=[col_blk, row_blk(aw), row_blk(aw), row_blk(aw), col_blk,
                   row_blk(aw), row_blk(aw)],
        out_shape=[jax.ShapeDtypeStruct((aw, rows), BF16),
                   jax.ShapeDtypeStruct((rows, aw), F32),
                   jax.ShapeDtypeStruct((rows, aw), BF16),
                   jax.ShapeDtypeStruct((rows, aw), F32),
                   jax.ShapeDtypeStruct((aw, rows), BF16),
                   jax.ShapeDtypeStruct((rows, aw), F32),
                   jax.ShapeDtypeStruct((rows, aw), F32)],
        compiler_params=pltpu.CompilerParams(
            dimension_semantics=("arbitrary",), vmem_limit_bytes=_vmem_limit(vmem)),
        name="in_proj",
    )(x, g, w_bf16, q_gain, k_gain, gsum, gexp)


def _lambda(lq_ref, lk_ref, lam_init):
    e = jnp.exp(jnp.sum(lq_ref[...] * lk_ref[...], axis=-1, keepdims=True))
    return e[0:1] - e[1:2] + lam_init


def _prompt_attn_kernel(qt_ref, k_ref, vt_ref, bias_ref, lq_ref, lk_ref, gs_ref,
                        o_ref, m_scr, l_scr, acc_scr, sa_scr, sb_scr, *, t, nd, lam_init):
    qi = pl.program_id(1)
    qt = qt_ref[...]
    row = lax.broadcasted_iota(jnp.int32, qt.shape, 0)
    zero = jnp.zeros_like(qt)
    qmaps = (jnp.where(row < QK_HEAD_DIM, qt, zero), jnp.where(row >= QK_HEAD_DIM, qt, zero))
    m_scr[...] = jnp.full(m_scr.shape, NEG_INF, F32)
    l_scr[...] = jnp.zeros(l_scr.shape, F32)
    acc_scr[...] = jnp.zeros(acc_scr.shape, F32)

    def scores(ki, s_scr):
        kb = k_ref[pl.ds(pl.multiple_of(ki * t, t), t), :]
        for c in range(2):
            s_scr[c] = _dot(kb, qmaps[c])

    def attend(ki, s_scr, delta):
        vt = vt_ref[:, pl.ds(pl.multiple_of(ki * t, t), t)]
        b = bias_ref[0, delta]
        for c in range(2):
            s = s_scr[c] + b
            m_prev = m_scr[c]
            m_new = jnp.maximum(m_prev, jnp.max(s, axis=0, keepdims=True))
            p = jnp.exp2(s - m_new)
            alpha = jnp.exp2(m_prev - m_new)
            l_scr[c] = alpha * l_scr[c] + jnp.sum(p, axis=0, keepdims=True)
            acc_scr[c] = alpha * acc_scr[c] + _dot(vt, p.astype(BF16))
            m_scr[c] = m_new

    scores(0, sa_scr)

    def body(j, carry):
        k0 = 2 * j
        scores(k0 + 1, sb_scr)
        attend(k0, sa_scr, jnp.minimum(qi - k0, nd))
        scores(k0 + 2, sa_scr)
        attend(k0 + 1, sb_scr, jnp.minimum(qi - k0 - 1, nd))
        return carry

    lax.fori_loop(0, qi // 2, body, 0)

    @pl.when(qi % 2 == 0)
    def _():
        attend(qi, sa_scr, 0)

    @pl.when(qi % 2 == 1)
    def _():
        scores(qi, sb_scr)
        attend(qi - 1, sa_scr, min(1, nd))
        attend(qi, sb_scr, 0)

    lam = _lambda(lq_ref, lk_ref, lam_init)
    o = acc_scr[0] / l_scr[0] - lam * (acc_scr[1] / l_scr[1])
    o = o * lax.rsqrt(jnp.mean(o * o, axis=0, keepdims=True) + RMS_EPS)
    o_ref[...] = (o.T * gs_ref[...] * (1.0 - lam_init)).astype(o_ref.dtype)


def _prompt_attention(qt, kb, vt, bias, lq, lk, g_sub, *, t, nd, lam_init):
    aw, s = qt.shape
    vmem = (2 * t * LANES * 2 + 2 * LANES * s * 2 + 2 * s * LANES * 2
            + 2 * (nd + 1) * t * t * 4 + 2 * t * LANES * 2
            + 2 * t * (2 * SUBLANES + LANES) * 4 + 4 * t * t * 4 + 8 * t * t * 4)
    return pl.pallas_call(
        functools.partial(_prompt_attn_kernel, t=t, nd=nd, lam_init=lam_init),
        grid=(N_HEADS, s // t),
        in_specs=[pl.BlockSpec((V_HEAD_DIM, t), lambda h, i: (h, i)),
                  pl.BlockSpec((s, V_HEAD_DIM), lambda h, i: (0, h)),
                  pl.BlockSpec((V_HEAD_DIM, s), lambda h, i: (h, 0)),
                  pl.BlockSpec((1, nd + 1, t, t), lambda h, i: (h, 0, 0, 0)),
                  pl.BlockSpec((2, QK_HEAD_DIM), lambda h, i: (0, 0)),
                  pl.BlockSpec((2, QK_HEAD_DIM), lambda h, i: (0, 0)),
                  pl.BlockSpec((1, V_HEAD_DIM), lambda h, i: (0, 0))],
        out_specs=pl.BlockSpec((t, V_HEAD_DIM), lambda h, i: (i, h)),
        out_shape=jax.ShapeDtypeStruct((s, aw), BF16),
        scratch_shapes=[pltpu.VMEM((2, 1, t), F32), pltpu.VMEM((2, 1, t), F32),
                        pltpu.VMEM((2, V_HEAD_DIM, t), F32),
                        pltpu.VMEM((2, t, t), F32), pltpu.VMEM((2, t, t), F32)],
        compiler_params=pltpu.CompilerParams(
            dimension_semantics=("arbitrary", "arbitrary"),
            vmem_limit_bytes=_vmem_limit(vmem)),
        name="prompt_attn",
    )(qt, kb, vt, bias, lq, lk, g_sub)


def _sample_attn_kernel(pt_ref, qr_ref, kn_ref, vn_ref, bias_ref, bself_ref, lq_ref, lk_ref,
                        gs_ref, *rest, pages, page, lam_init):
    del pt_ref
    k_refs = rest[:pages]
    v_refs = rest[pages:2 * pages]
    o_ref, m_scr, l_scr, acc_scr = rest[2 * pages:]
    step = pl.program_id(1)
    qr = qr_ref[0]
    half = qr.shape[0] // 2

    def both_maps(b):
        return jnp.concatenate([b, b], axis=0)

    def update(s, vs):
        m_prev = m_scr[...]
        m_new = jnp.maximum(m_prev, jnp.max(s, axis=-1, keepdims=True))
        p = jnp.exp2(s - m_new)
        alpha = jnp.exp2(m_prev - m_new)
        l_scr[...] = alpha * l_scr[...] + jnp.sum(p, axis=-1, keepdims=True)
        p = p.astype(BF16)
        acc = alpha * acc_scr[...]
        for j, vj in enumerate(vs):
            acc = acc + _dot(p[:, j * page:(j + 1) * page], vj)
        acc_scr[...] = acc
        m_scr[...] = m_new

    @pl.when(step == 0)
    def _():
        m_scr[...] = jnp.full(m_scr.shape, NEG_INF, F32)
        l_scr[...] = jnp.zeros(l_scr.shape, F32)
        acc_scr[...] = jnp.zeros(acc_scr.shape, F32)
        update(_dot(qr, kn_ref[0]) + both_maps(bself_ref[...]), [vn_ref[0]])

    def v_page(ref):
        return jnp.concatenate(
            [ref[0, pl.ds(h, page, stride=N_HEADS), :] for h in range(N_HEADS)],
            axis=1).astype(BF16)

    s = jnp.concatenate(
        [_dot(qr, k_refs[j][0].astype(BF16)) + both_maps(bias_ref[j]) for j in range(pages)],
        axis=1)
    update(s, [v_page(v_refs[j]) for j in range(pages)])

    @pl.when(step == pl.num_programs(1) - 1)
    def _():
        o = acc_scr[...] / l_scr[...]
        lam = _lambda(lq_ref, lk_ref, lam_init)
        o = o[:half] - lam * o[half:]
        row_head = lax.broadcasted_iota(jnp.int32, o.shape, 0) % N_HEADS
        lane_head = lax.broadcasted_iota(jnp.int32, o.shape, 1) // V_HEAD_DIM
        o = jnp.where(row_head == lane_head, o, 0.0)
        ms = jnp.sum(o * o, axis=-1, keepdims=True) * (1.0 / V_HEAD_DIM)
        o = o * lax.rsqrt(ms + RMS_EPS)
        o = jnp.sum(o.reshape(half // N_HEADS, N_HEADS, o.shape[1]), axis=1)
        o_ref[0] = (o * gs_ref[...] * (1.0 - lam_init)).astype(o_ref.dtype)


def _sample_attention(page_table, q_rows, kt_new, v_new, bias_pages, bias_self, lq, lk, g_sub_t,
                      cache_kt, cache_vr, *, pages, lam_init):
    nb, n_pages = page_table.shape
    _, aw, page = cache_kt.shape
    dt = q_rows.shape[1] // (2 * N_HEADS)
    steps = n_pages // pages
    assert n_pages % pages == 0
    rows = q_rows.shape[1]

    def page_spec(shape):
        return [pl.BlockSpec((1,) + shape,
                             lambda b, s, pt, j=j: (pt[b * n_pages + s * pages + j], 0, 0))
                for j in range(pages)]

    per_b = lambda shape: pl.BlockSpec(shape, lambda b, s, pt: (b, 0, 0))
    const2 = lambda shape: pl.BlockSpec(shape, lambda b, s, pt: (0, 0))
    vmem = (2 * 2 * pages * page * aw * 4 + 2 * pages * page * aw * 2
            + 4 * rows * aw * 4 + 2 * 3 * rows * aw * 2)
    grid_spec = pltpu.PrefetchScalarGridSpec(
        num_scalar_prefetch=1,
        grid=(nb, steps),
        in_specs=[per_b((1, rows, aw)), per_b((1, aw, page)), per_b((1, page, aw)),
                  pl.BlockSpec((pages, rows // 2, page), lambda b, s, pt: (s, 0, 0)),
                  const2((rows // 2, page)),
                  const2((2, QK_HEAD_DIM)), const2((2, QK_HEAD_DIM)), const2((1, aw))]
                 + page_spec((aw, page)) + page_spec((page * N_HEADS, V_HEAD_DIM)),
        out_specs=pl.BlockSpec((1, dt, aw), lambda b, s, pt: (b, 0, 0)),
        scratch_shapes=[pltpu.VMEM((rows, 1), F32), pltpu.VMEM((rows, 1), F32),
                        pltpu.VMEM((rows, aw), F32)])
    return pl.pallas_call(
        functools.partial(_sample_attn_kernel, pages=pages, page=page, lam_init=lam_init),
        grid_spec=grid_spec,
        out_shape=jax.ShapeDtypeStruct((nb, dt, aw), BF16),
        compiler_params=pltpu.CompilerParams(
            dimension_semantics=("arbitrary", "arbitrary"),
            vmem_limit_bytes=_vmem_limit(vmem)),
        name="sample_attn",
    )(page_table.reshape(-1), q_rows, kt_new, v_new, bias_pages, bias_self, lq, lk, g_sub_t,
      *([cache_kt] * pages), *([cache_vr] * pages))


def _out_proj_kernel(*refs, halo):
    if halo:
        a_ref, gb_ref, u_ref, prev_ref, cw_ref, wo_ref, x_ref, o_ref = refs
        u = u_ref[...]
        first = pl.program_id(0) == 0
        prev = jnp.where(first, 0.0, prev_ref[...])
        ucat = jnp.concatenate([prev, u], axis=0)
        p1 = pltpu.roll(ucat, 1, axis=0)[SUBLANES:]
        p2 = pltpu.roll(ucat, 2, axis=0)[SUBLANES:]
    else:
        a_ref, gb_ref, u_ref, p1_ref, p2_ref, cw_ref, wo_ref, x_ref, o_ref = refs
        u, p1, p2 = u_ref[...], p1_ref[...], p2_ref[...]
    cw = cw_ref[...]
    aw = u.shape[1]
    c = gb_ref[...] * (p2 * cw[0:1] + p1 * cw[1:2] + u * cw[2:3])
    y = _dot(a_ref[...], wo_ref[:aw, :]) + _dot(c.astype(BF16), wo_ref[aw:, :])
    o_ref[...] = x_ref[...] + y


def _out_project(a, gb, u, conv_w, wo_bf16, x, prev=None):
    rows, d = x.shape
    aw = a.shape[1]
    tm = min(rows, 256)
    assert rows % tm == 0
    row_blk = lambda width: pl.BlockSpec((tm, width), lambda i: (i, 0))
    const = lambda shape: pl.BlockSpec(shape, lambda i: (0, 0))
    halo = prev is None
    if halo:
        nsub = tm // SUBLANES
        extra = [pl.BlockSpec((SUBLANES, aw), lambda i: (jnp.maximum(i * nsub - 1, 0), 0))]
        extra_args = [u]
    else:
        extra = [row_blk(aw), row_blk(aw)]
        extra_args = list(prev)
    vmem = 2 * d * d * 2 + 2 * tm * (aw * (2 + 4 + 4 + 8) + d * 8) + 6 * tm * d * 4
    return pl.pallas_call(
        functools.partial(_out_proj_kernel, halo=halo),
        grid=(rows // tm,),
        in_specs=[row_blk(aw), row_blk(aw), row_blk(aw)] + extra
                 + [const((CONV_K, aw)), const((d, d)), row_blk(d)],
        out_specs=row_blk(d),
        out_shape=jax.ShapeDtypeStruct((rows, d), F32),
        compiler_params=pltpu.CompilerParams(
            dimension_semantics=("arbitrary",), vmem_limit_bytes=_vmem_limit(vmem)),
        name="out_proj",
    )(a, gb, u, *extra_args, conv_w, wo_bf16, x)


ROUTER_ROWS = 32
EXPERT_ROW0 = 8


def _router_kernel(x_ref, g_ref, w3_ref, bias_ref, idx_ref, wt_ref):
    x = x_ref[...]
    ms = jnp.mean(x * x, axis=-1, keepdims=True)
    xn = x * lax.rsqrt(ms + RMS_EPS) * g_ref[...]
    hi, lo = _split_bf16(xn)
    logits = _dot_nt(w3_ref[...], jnp.concatenate([hi, hi, lo], axis=1)) + bias_ref[...]
    big = jnp.int32(1 << 20)

    gl = logits[0:SUBLANES]
    grow = lax.broadcasted_iota(jnp.int32, gl.shape, 0)
    gmax = jnp.max(gl, axis=0, keepdims=True)
    g_idx = jnp.min(jnp.where(gl == gmax, grow, big), axis=0, keepdims=True)
    g_w = 1.0 / jnp.sum(jnp.exp(gl - gmax), axis=0, keepdims=True)

    el = logits[EXPERT_ROW0:EXPERT_ROW0 + N_EXPERTS]
    erow = lax.broadcasted_iota(jnp.int32, el.shape, 0)
    el = jnp.where(erow // EXPERTS_PER_GROUP == g_idx, el, NEG_INF)
    v1 = jnp.max(el, axis=0, keepdims=True)
    i1 = jnp.min(jnp.where(el == v1, erow, big), axis=0, keepdims=True)
    el2 = jnp.where(erow == i1, NEG_INF, el)
    v2 = jnp.max(el2, axis=0, keepdims=True)
    i2 = jnp.min(jnp.where(el2 == v2, erow, big), axis=0, keepdims=True)
    t = jnp.exp(v2 - v1)
    w1 = g_w / (1.0 + t)
    idx_ref[...] = jnp.concatenate([i1, i2], axis=0)
    wt_ref[...] = jnp.concatenate([w1, w1 * t], axis=0)


def _route(x, g, w3, bias_col):
    rows, d = x.shape
    tm = math.gcd(rows, 256)
    assert tm % LANES == 0
    return pl.pallas_call(
        _router_kernel,
        grid=(rows // tm,),
        in_specs=[pl.BlockSpec((tm, d), lambda i: (i, 0)),
                  pl.BlockSpec((1, d), lambda i: (0, 0)),
                  pl.BlockSpec((ROUTER_ROWS, 3 * d), lambda i: (0, 0)),
                  pl.BlockSpec((ROUTER_ROWS, 1), lambda i: (0, 0))],
        out_specs=[pl.BlockSpec((2, tm), lambda i: (0, i)),
                   pl.BlockSpec((2, tm), lambda i: (0, i))],
        out_shape=[jax.ShapeDtypeStruct((2, rows), jnp.int32),
                   jax.ShapeDtypeStruct((2, rows), F32)],
        compiler_params=pltpu.CompilerParams(dimension_semantics=("arbitrary",)),
        name="router",
    )(x, g, w3, bias_col)


EXPERT_PHASES = 4
EXPERT_TILE = 128


def _expert_kernel(te_ref, src_cur, src_next, dst_prev, dst_cur, roww_ref, g_ref,
                   wg_ref, wu_ref, wd_ref, x_hbm, y_hbm,
                   xbuf, ybuf, wgb, wub, wdb, gsem, ssem, *, tm):
    j = pl.program_id(0)
    last = pl.num_programs(0) - 1
    slot = j % 2
    other = 1 - slot

    def gather_rows(idx_ref, buf, lo, hi):
        for r in range(lo, hi):
            pltpu.make_async_copy(x_hbm.at[pl.ds(idx_ref[0, 0, r], 1)],
                                  xbuf.at[buf, pl.ds(r, 1)], gsem.at[buf]).start()

    def scatter_rows(idx_ref, buf, lo, hi):
        for r in range(lo, hi):
            pltpu.make_async_copy(ybuf.at[buf, pl.ds(r, 1)],
                                  y_hbm.at[pl.ds(idx_ref[0, 0, r], 1)], ssem.at[buf]).start()

    def wait_gather(buf):
        pltpu.make_async_copy(x_hbm.at[pl.ds(0, tm)], xbuf.at[buf], gsem.at[buf]).wait()

    def wait_scatter(buf):
        pltpu.make_async_copy(ybuf.at[buf], y_hbm.at[pl.ds(0, tm)], ssem.at[buf]).wait()

    @pl.when(j == 0)
    def _():
        ybuf[1] = jnp.zeros(ybuf.shape[1:], F32)
        gather_rows(src_cur, 0, 0, tm)

    @pl.when(jnp.logical_or(j == 0, te_ref[j] != te_ref[jnp.maximum(j - 1, 0)]))
    def _():
        wgb[...] = wg_ref[0].astype(BF16)
        wub[...] = wu_ref[0].astype(BF16)
        wdb[...] = wd_ref[0].astype(BF16)

    wait_gather(slot)

    @pl.when(j >= 1)
    def _():
        wait_scatter(slot)

    x = xbuf[slot]
    ms = jnp.mean(x * x, axis=-1, keepdims=True)
    xn = (x * lax.rsqrt(ms + RMS_EPS) * g_ref[...]).astype(BF16)
    roww = roww_ref[:, 0:1]
    resid = roww_ref[:, 1:2]
    ff = wgb.shape[1]
    d = wdb.shape[1]
    step = tm // EXPERT_PHASES
    half_ff, half_d = ff // 2, d // 2
    hs = []
    for c in range(2):
        gather_rows(src_next, other, (2 * c) * step, (2 * c + 2) * step)
        cols = slice(c * half_ff, (c + 1) * half_ff)
        hg = _dot(xn, wgb[:, cols])
        hu = _dot(xn, wub[:, cols])
        hs.append((hg * jax.nn.sigmoid(hg) * hu * roww).astype(BF16))
    h = jnp.concatenate(hs, axis=1)
    for c in range(2):
        scatter_rows(dst_prev, other, (2 * c) * step, (2 * c + 2) * step)
        cols = slice(c * half_d, (c + 1) * half_d)
        ybuf[slot, :, cols] = _dot(h, wdb[:, cols]) + x[:, cols] * resid

    @pl.when(j == last)
    def _():
        scatter_rows(dst_cur, slot, 0, tm)
        wait_gather(other)
        wait_scatter(other)
        wait_scatter(slot)


def _experts(tile_expert, src_cur, src_next, dst_prev, dst_cur, row_w, g, w_gate, w_up, w_down,
             x, *, tm, out_rows):
    nt = tile_expert.shape[0]
    d = x.shape[1]
    ff = w_gate.shape[2]
    assert tm % EXPERT_PHASES == 0
    smem_row = pl.BlockSpec((1, 1, tm), lambda i, te: (i, 0, 0), memory_space=pltpu.SMEM)
    expert_w = lambda shape: pl.BlockSpec((1,) + shape, lambda i, te: (te[i], 0, 0))
    grid_spec = pltpu.PrefetchScalarGridSpec(
        num_scalar_prefetch=1,
        grid=(nt,),
        in_specs=[smem_row, smem_row, smem_row, smem_row,
                  pl.BlockSpec((tm, 2), lambda i, te: (i, 0)),
                  pl.BlockSpec((1, d), lambda i, te: (0, 0)),
                  expert_w((d, ff)), expert_w((d, ff)), expert_w((ff, d)),
                  pl.BlockSpec(memory_space=pl.ANY)],
        out_specs=pl.BlockSpec(memory_space=pl.ANY),
        scratch_shapes=[pltpu.VMEM((2, tm, d), F32), pltpu.VMEM((2, tm, d), F32),
                        pltpu.VMEM((d, ff), BF16), pltpu.VMEM((d, ff), BF16),
                        pltpu.VMEM((ff, d), BF16),
                        pltpu.SemaphoreType.DMA((2,)), pltpu.SemaphoreType.DMA((2,))])
    vmem = 2 * 3 * d * ff * 4 + 3 * d * ff * 2 + 4 * tm * d * 4 + 6 * tm * d * 4
    return pl.pallas_call(
        functools.partial(_expert_kernel, tm=tm),
        grid_spec=grid_spec,
        out_shape=jax.ShapeDtypeStruct((out_rows, d), F32),
        compiler_params=pltpu.CompilerParams(
            dimension_semantics=("arbitrary",), vmem_limit_bytes=_vmem_limit(vmem)),
        name="experts",
    )(tile_expert, src_cur, src_next, dst_prev, dst_cur, row_w, g, w_gate, w_up, w_down, x)


def _combine_kernel(y0_ref, y1_ref, o_ref):
    o_ref[...] = y0_ref[...] + y1_ref[...]


def _combine(y2, row0, rows, total_rows):
    d = y2.shape[1]
    tm = math.gcd(math.gcd(rows, row0) if row0 else rows, LANES)
    assert total_rows % tm == 0 and tm % SUBLANES == 0
    b0 = row0 // tm
    b1 = (total_rows + row0) // tm
    return pl.pallas_call(
        _combine_kernel,
        grid=(rows // tm,),
        in_specs=[pl.BlockSpec((tm, d), lambda i: (b0 + i, 0)),
                  pl.BlockSpec((tm, d), lambda i: (b1 + i, 0))],
        out_specs=pl.BlockSpec((tm, d), lambda i: (i, 0)),
        out_shape=jax.ShapeDtypeStruct((rows, d), F32),
        compiler_params=pltpu.CompilerParams(dimension_semantics=("arbitrary",)),
        name="moe_combine",
    )(y2, y2)


def _routing_plan(e_idx, w, tm):
    total = e_idx.shape[1]
    n_assign = 2 * total
    nt = -(-n_assign // tm) + N_EXPERTS
    e_flat = e_idx.reshape(-1)
    w_flat = w.reshape(-1)
    order = jnp.argsort(e_flat, stable=True).astype(jnp.int32)
    counts = jnp.sum((e_flat[:, None] == jnp.arange(N_EXPERTS, dtype=jnp.int32)[None, :])
                     .astype(jnp.int32), axis=0)
    tiles_per = (counts + tm - 1) // tm
    tile_end = jnp.cumsum(tiles_per)
    tile_start = tile_end - tiles_per
    group_start = jnp.cumsum(counts) - counts
    tiles = jnp.arange(nt, dtype=jnp.int32)
    tile_expert = jnp.minimum(
        jnp.sum((tiles[:, None] >= tile_end[None, :]).astype(jnp.int32), axis=1), N_EXPERTS - 1)
    tile_valid = (tiles < tile_end[-1]).astype(jnp.int32)
    rank = (tiles - tile_start[tile_expert])[:, None] * tm + jnp.arange(tm, dtype=jnp.int32)[None, :]
    valid = (rank < counts[tile_expert][:, None]) & (tile_valid[:, None] > 0)
    pos = jnp.clip(group_start[tile_expert][:, None] + rank, 0, n_assign - 1)
    assign = order[pos]
    pad_rank = jnp.cumsum(jnp.logical_not(valid).reshape(-1).astype(jnp.int32)) - 1
    dump = n_assign + pad_rank.reshape(nt, tm)
    row_src = jnp.where(valid, assign % total, 0)
    row_dst = jnp.where(valid, assign, dump)
    row_w = jnp.stack([jnp.where(valid, w_flat[assign], 0.0),
                       (valid & (assign < total)).astype(F32)], axis=-1)
    src_next = jnp.concatenate([row_src[1:], row_src[-1:]], axis=0)
    first_dst = nt * tm + jnp.arange(tm, dtype=jnp.int32)[None, :]
    dst_prev = jnp.concatenate([first_dst, row_dst[:-1]], axis=0)
    as_smem = lambda a: a.reshape(nt, 1, tm)
    return (tile_expert, as_smem(row_src), as_smem(src_next), as_smem(dst_prev), as_smem(row_dst),
            row_w.reshape(nt * tm, 2)), (nt + 1) * tm


def _group_matrices(aw):
    e = jnp.arange(aw, dtype=jnp.int32) // QK_HEAD_DIM
    lanes = jnp.arange(LANES, dtype=jnp.int32)
    member = (e[:, None] == lanes[None, :]).astype(BF16)
    return jnp.concatenate([member, member], axis=0), jnp.concatenate([member.T, member.T], axis=0)


def _toeplitz_kernel(strip_ref, out_ref, *, t):
    x = jnp.broadcast_to(strip_ref[0], (t, 2 * t))
    out_ref[0, 0] = pltpu.roll(x, t + 1, axis=1, stride=1, stride_axis=0)[:, :t]


def _prompt_bias(rel_table, t, nd):
    m = jnp.arange(2 * t, dtype=jnp.int32)
    dist = jnp.arange(nd + 1, dtype=jnp.int32)[:, None] * t - (t - 1) + m[None, :]
    strips = _bias_table(rel_table, dist.reshape(-1, LANES))
    strips = strips.reshape(N_HEADS * (nd + 1), 1, 2 * t)
    return pl.pallas_call(
        functools.partial(_toeplitz_kernel, t=t),
        grid=(N_HEADS, nd + 1),
        in_specs=[pl.BlockSpec((1, 1, 2 * t), lambda h, dl: (h * (nd + 1) + dl, 0, 0))],
        out_specs=pl.BlockSpec((1, 1, t, t), lambda h, dl: (h, dl, 0, 0)),
        out_shape=jax.ShapeDtypeStruct((N_HEADS, nd + 1, t, t), F32),
        name="bias_toeplitz",
    )(strips)


def _sample_bias(rel_table, past, n_pages, page, dt):
    tq = jnp.arange(dt, dtype=jnp.int32)
    kpos = jnp.arange(n_pages * page, dtype=jnp.int32).reshape(n_pages, 1, page)
    dist_pages = (past + tq[None, :, None] - kpos).reshape(-1, page)
    j = jnp.arange(page, dtype=jnp.int32)
    self_rows = jnp.where(j[None, :] < dt, tq[:, None] - j[None, :], -1)
    dist = jnp.concatenate([dist_pages, self_rows], axis=0)
    rows = dist.shape[0]
    rpad = (-rows) % (256 if rows > 256 else SUBLANES)
    dist = jnp.concatenate([dist, jnp.full((rpad, page), -1, jnp.int32)], axis=0)
    bias = _bias_table(rel_table, dist)
    bp = bias[:, :n_pages * dt].reshape(N_HEADS, n_pages, dt, page)
    bp = bp.transpose(1, 2, 0, 3).reshape(n_pages, dt * N_HEADS, page)
    bs = bias[:, n_pages * dt:n_pages * dt + dt].transpose(1, 0, 2).reshape(dt * N_HEADS, page)
    return bp, bs


def kernel(x_prompt, x_sample, cache_k, cache_v, state_conv, page_table, rel_table, attn_norm_g,
           w_in, q_norm_g, k_norm_g, lambda_q, lambda_k, sub_norm_g, conv_w, w_o, ffn_norm_g,
           router_group_w, router_group_b, router_expert_w, router_expert_b, w_gate, w_up, w_down):
    depth = w_in.shape[0]
    assert depth == 1, "single-layer trunk"
    batch, seq, d = x_prompt.shape
    assert batch == 1, "one prompt sequence"
    nb, dt, _ = x_sample.shape
    n_pool, page = cache_k.shape[1], cache_k.shape[2]
    n_pages = page_table.shape[1]
    past = n_pages * page
    aw = N_HEADS * V_HEAD_DIM
    layer = 0
    lam_init = 0.8 - 0.6 * math.exp(-0.3 * layer)

    w_in_b = w_in[layer].astype(BF16)
    w_o_b = w_o[layer].astype(BF16)
    g_attn = attn_norm_g[layer].reshape(1, d)
    g_ffn = ffn_norm_g[layer].reshape(1, d)
    qg = jnp.tile(q_norm_g[layer], aw // QK_HEAD_DIM).reshape(1, aw)
    kg = jnp.tile(k_norm_g[layer], aw // QK_HEAD_DIM).reshape(1, aw)
    g_sub = sub_norm_g[layer].reshape(1, V_HEAD_DIM)
    g_sub_t = jnp.tile(sub_norm_g[layer], N_HEADS).reshape(1, aw)
    gsum, gexp = _group_matrices(aw)
    lq, lk = lambda_q[layer], lambda_k[layer]
    cw = conv_w[layer]

    xp = x_prompt.reshape(seq, d)
    xs = x_sample.reshape(nb * dt, d)

    qtp, kp, kbp, vp, vtp, gbp, up = _project(xp, g_attn, w_in_b, qg, kg, gsum, gexp)
    qts, ks, _, vs, _, gbs, us = _project(xs, g_attn, w_in_b, qg, kg, gsum, gexp)

    t = min(seq, 512)
    assert seq % t == 0
    nd = min(-(-(MAX_DISTANCE + t - 1) // t), seq // t)
    bias_p = _prompt_bias(rel_table, t, nd)
    ap = _prompt_attention(qtp, kbp, vtp, bias_p, lq, lk, g_sub, t=t, nd=nd, lam_init=lam_init)

    bias_pages, bias_self = _sample_bias(rel_table, past, n_pages, page, dt)
    lane_hc = jnp.arange(aw, dtype=jnp.int32) // QK_HEAD_DIM
    row_c = jnp.arange(2, dtype=jnp.int32)[:, None, None]
    row_h = jnp.arange(N_HEADS, dtype=jnp.int32)[None, None, :]
    sel = (lane_hc[None, None, None, :] == (row_h * 2 + row_c)[..., None])
    q3 = qts.T.reshape(nb, 1, dt, 1, aw)
    q_rows = jnp.where(sel[None], q3, jnp.zeros((), BF16)).reshape(nb, 2 * dt * N_HEADS, aw)
    k3 = ks.reshape(nb, dt, aw).astype(BF16)
    kt_new = jnp.pad(k3.transpose(0, 2, 1), ((0, 0), (0, 0), (0, page - dt)))
    v_new = jnp.pad(vs.reshape(nb, dt, aw).astype(BF16), ((0, 0), (0, page - dt), (0, 0)))
    cache_kt = cache_k[layer].transpose(0, 2, 3, 4, 1).reshape(n_pool, aw, page)
    cache_vr = cache_v[layer].reshape(n_pool, page * N_HEADS, V_HEAD_DIM)
    pages_per_step = math.gcd(n_pages, PAGES_PER_STEP)
    a_s = _sample_attention(page_table, q_rows, kt_new, v_new, bias_pages, bias_self,
                            lq, lk, g_sub_t, cache_kt, cache_vr,
                            pages=pages_per_step, lam_init=lam_init)
    a_s = a_s.reshape(nb * dt, aw)

    x2p = _out_project(ap, gbp, up, cw, w_o_b, xp)
    u_ext = jnp.concatenate([state_conv[layer].astype(F32), us.reshape(nb, dt, aw)], axis=1)
    prev1 = u_ext[:, 1:1 + dt].reshape(nb * dt, aw)
    prev2 = u_ext[:, 0:dt].reshape(nb * dt, aw)
    x2s = _out_project(a_s, gbs, us, cw, w_o_b, xs, prev=(prev1, prev2))
    x2 = jnp.concatenate([x2p, x2s], axis=0)
    total = seq + nb * dt

    wr = jnp.zeros((ROUTER_ROWS, d), F32)
    wr = wr.at[0:N_GROUPS].set(router_group_w[layer].T)
    wr = wr.at[EXPERT_ROW0:EXPERT_ROW0 + N_EXPERTS].set(router_expert_w[layer].T)
    wr_hi, wr_lo = _split_bf16(wr)
    w3 = jnp.concatenate([wr_hi, wr_lo, wr_hi], axis=1)
    bias_col = jnp.full((ROUTER_ROWS,), NEG_INF, F32)
    bias_col = bias_col.at[0:N_GROUPS].set(router_group_b[layer].astype(F32))
    bias_col = bias_col.at[EXPERT_ROW0:EXPERT_ROW0 + N_EXPERTS].set(
        router_expert_b[layer].reshape(-1).astype(F32)).reshape(ROUTER_ROWS, 1)
    e_idx, wts = _route(x2, g_ffn, w3, bias_col)
    tm_e = EXPERT_TILE
    plan, y_rows = _routing_plan(e_idx, wts, tm_e)
    y2 = _experts(*plan, g_ffn, w_gate[layer], w_up[layer], w_down[layer], x2,
                  tm=tm_e, out_rows=y_rows)
    yp = _combine(y2, 0, seq, total)
    ys = _combine(y2, seq, nb * dt, total)

    return (yp.reshape(batch, seq, d),
            ys.reshape(nb, dt, d),
            kp.reshape(depth, batch, seq, N_HEADS, 2, QK_HEAD_DIM),
            vp.reshape(depth, batch, seq, N_HEADS, V_HEAD_DIM),
            up[seq - (CONV_K - 1):].reshape(depth, batch, CONV_K - 1, aw),
            ks.reshape(depth, nb, dt, N_HEADS, 2, QK_HEAD_DIM),
            vs.reshape(depth, nb, dt, N_HEADS, V_HEAD_DIM),
            u_ext[:, dt:].reshape(depth, nb, CONV_K - 1, aw))
```

```python
import functools
import math

import jax
import jax.numpy as jnp
from jax import lax
from jax.experimental import pallas as pl
from jax.experimental.pallas import tpu as pltpu

N_HEADS = 8
V_HEAD_DIM = 128
QK_HEAD_DIM = 64
CONV_K = 3
N_BUCKETS = 32
MAX_EXACT = N_BUCKETS // 2
MAX_DISTANCE = 2048
N_GROUPS = 4
EXPERTS_PER_GROUP = 4
N_EXPERTS = N_GROUPS * EXPERTS_PER_GROUP
RMS_EPS = 1e-6
NEG_INF = -1e30
LOG2E = math.log2(math.e)

LANES = 128
SUBLANES = 8
VMEM_PHYSICAL_BYTES = 64 * 1024 * 1024
PAGES_PER_STEP = 16

F32 = jnp.float32
BF16 = jnp.bfloat16


def _vmem_limit(nbytes):
    return int(min(max(nbytes * 5 // 4 + (4 << 20), 16 << 20), VMEM_PHYSICAL_BYTES - (6 << 20)))


def _dot(a, b):
    return jnp.dot(a, b, preferred_element_type=F32)


def _dot_nt(a, b):
    return lax.dot_general(a, b, (((1,), (1,)), ((), ())), preferred_element_type=F32)


def _split_bf16(x):
    hi = x.astype(BF16)
    lo = (x - hi.astype(F32)).astype(BF16)
    return hi, lo


def _bias_kernel(table_ref, dist_ref, out_ref):
    d = dist_ref[...]
    n = jnp.maximum(d, 0)
    nf = jnp.maximum(n, 1).astype(F32)
    large = MAX_EXACT + (jnp.log(nf / MAX_EXACT) / math.log(MAX_DISTANCE / MAX_EXACT)
                         * (N_BUCKETS - MAX_EXACT)).astype(jnp.int32)
    large = jnp.minimum(large, N_BUCKETS - 1)
    bucket = jnp.where(n < MAX_EXACT, n, large)
    hits = [bucket == b for b in range(N_BUCKETS)]
    for h in range(N_HEADS):
        acc = jnp.zeros(d.shape, F32)
        for b in range(N_BUCKETS):
            acc = jnp.where(hits[b], table_ref[b, h] * LOG2E, acc)
        out_ref[h] = jnp.where(d < 0, NEG_INF, acc)


def _bias_table(rel_table, dist):
    rows = dist.shape[0]
    tr = min(rows, 256)
    assert rows % tr == 0
    return pl.pallas_call(
        _bias_kernel,
        grid=(rows // tr,),
        in_specs=[pl.BlockSpec(memory_space=pltpu.SMEM),
                  pl.BlockSpec((tr, LANES), lambda i: (i, 0))],
        out_specs=pl.BlockSpec((N_HEADS, tr, LANES), lambda i: (0, i, 0)),
        out_shape=jax.ShapeDtypeStruct((N_HEADS, rows, LANES), F32),
        name="rel_bias",
    )(rel_table.astype(F32), dist)


def _group_rms_scale(z, gsum, gexp):
    hi, lo = _split_bf16(z * z)
    ssq = _dot(jnp.concatenate([hi, lo], axis=1), gsum)
    r = lax.rsqrt(ssq * (1.0 / QK_HEAD_DIM) + RMS_EPS)
    r_hi, r_lo = _split_bf16(r)
    return _dot(jnp.concatenate([r_hi, r_lo], axis=1), gexp)


def _proj_kernel(x_ref, g_ref, w_ref, qg_ref, kg_ref, gsum_ref, gexp_ref,
                 qt_ref, kf_ref, kb_ref, vf_ref, vt_ref, gb_ref, u_ref, *, aw):
    x = x_ref[...]
    ms = jnp.mean(x * x, axis=-1, keepdims=True)
    xn = (x * lax.rsqrt(ms + RMS_EPS) * g_ref[...]).astype(BF16)
    gsum = gsum_ref[...]
    gexp = gexp_ref[...]

    zq = _dot(xn, w_ref[:, 0 * aw:1 * aw])
    q = zq * _group_rms_scale(zq, gsum, gexp) * qg_ref[...]
    qt_ref[...] = (q * (QK_HEAD_DIM ** -0.5 * LOG2E)).T.astype(BF16)

    zk = _dot(xn, w_ref[:, 1 * aw:2 * aw])
    k = zk * _group_rms_scale(zk, gsum, gexp) * kg_ref[...]
    kf_ref[...] = k
    kb_ref[...] = k.astype(BF16)

    v = _dot(xn, w_ref[:, 2 * aw:3 * aw])
    vf_ref[...] = v
    vt_ref[...] = v.T.astype(BF16)

    gb_ref[...] = _dot(xn, w_ref[:, 3 * aw:4 * aw])
    u_ref[...] = _dot(xn, w_ref[:, 4 * aw:5 * aw]) * _dot(xn, w_ref[:, 5 * aw:6 * aw])


def _project(x, g, w_bf16, q_gain, k_gain, gsum, gexp):
    rows, d = x.shape
    pw = w_bf16.shape[1]
    aw = pw // 6
    tm = min(rows, 256)
    assert rows % tm == 0 and aw % LANES == 0
    row_blk = lambda width: pl.BlockSpec((tm, width), lambda i: (i, 0))
    col_blk = pl.BlockSpec((aw, tm), lambda i: (0, i))
    const = lambda shape: pl.BlockSpec(shape, lambda i: (0, 0))
    vmem = (d * pw * 2 + 2 * tm * d * 4 + 2 * tm * aw * (2 + 4 + 2 + 4 + 2 + 4 + 4)
            + 8 * tm * aw * 4 + 4 * aw * (2 * LANES + 2 * LANES))
    return pl.pallas_call(
        functools.partial(_proj_kernel, aw=aw),
        grid=(rows // tm,),
        in_specs=[row_blk(d), const((1, d)),
                  pl.BlockSpec((d, pw), lambda i: (0, 0), pipeline_mode=pl.Buffered(1)),
                  const((1, aw)), const((1, aw)),
                  const((2 * aw, LANES)), const((2 * LANES, aw))],
        out_specs=[col_blk, row_blk(aw), row_blk(aw), row_blk(aw), col_blk,
                   row_blk(aw), row_blk(aw)],
        out_shape=[jax.ShapeDtypeStruct((aw, rows), BF16),
                   jax.ShapeDtypeStruct((rows, aw), F32),
                   jax.ShapeDtypeStruct((rows, aw), BF16),
                   jax.ShapeDtypeStruct((rows, aw), F32),
                   jax.ShapeDtypeStruct((aw, rows), BF16),
                   jax.ShapeDtypeStruct((rows, aw), F32),
                   jax.ShapeDtypeStruct((rows, aw), F32)],
        compiler_params=pltpu.CompilerParams(
            dimension_semantics=("arbitrary",), vmem_limit_bytes=_vmem_limit(vmem)),
        name="in_proj",
    )(x, g, w_bf16, q_gain, k_gain, gsum, gexp)


def _lambda(lq_ref, lk_ref, lam_init):
    e = jnp.exp(jnp.sum(lq_ref[...] * lk_ref[...], axis=-1, keepdims=True))
    return e[0:1] - e[1:2] + lam_init


def _prompt_attn_kernel(qt_ref, k_ref, vt_ref, bias_ref, lq_ref, lk_ref, gs_ref,
                        o_ref, m_scr, l_scr, acc_scr, sa_scr, sb_scr, *, t, nd, lam_init):
    qi = pl.program_id(1)
    qt = qt_ref[...]
    row = lax.broadcasted_iota(jnp.int32, qt.shape, 0)
    zero = jnp.zeros_like(qt)
    qmaps = (jnp.where(row < QK_HEAD_DIM, qt, zero), jnp.where(row >= QK_HEAD_DIM, qt, zero))
    m_scr[...] = jnp.full(m_scr.shape, NEG_INF, F32)
    l_scr[...] = jnp.zeros(l_scr.shape, F32)
    acc_scr[...] = jnp.zeros(acc_scr.shape, F32)

    def scores(ki, s_scr):
        kb = k_ref[pl.ds(pl.multiple_of(ki * t, t), t), :]
        for c in range(2):
            s_scr[c] = _dot(kb, qmaps[c])

    def attend(ki, s_scr, delta):
        vt = vt_ref[:, pl.ds(pl.multiple_of(ki * t, t), t)]
        b = bias_ref[0, delta]
        for c in range(2):
            s = s_scr[c] + b
            m_prev = m_scr[c]
            m_new = jnp.maximum(m_prev, jnp.max(s, axis=0, keepdims=True))
            p = jnp.exp2(s - m_new)
            alpha = jnp.exp2(m_prev - m_new)
            l_scr[c] = alpha * l_scr[c] + jnp.sum(p, axis=0, keepdims=True)
            acc_scr[c] = alpha * acc_scr[c] + _dot(vt, p.astype(BF16))
            m_scr[c] = m_new

    scores(0, sa_scr)

    def body(j, carry):
        k0 = 2 * j
        scores(k0 + 1, sb_scr)
        attend(k0, sa_scr, jnp.minimum(qi - k0, nd))
        scores(k0 + 2, sa_scr)
        attend(k0 + 1, sb_scr, jnp.minimum(qi - k0 - 1, nd))
        return carry

    lax.fori_loop(0, qi // 2, body, 0)

    @pl.when(qi % 2 == 0)
    def _():
        attend(qi, sa_scr, 0)

    @pl.when(qi % 2 == 1)
    def _():
        scores(qi, sb_scr)
        attend(qi - 1, sa_scr, min(1, nd))
        attend(qi, sb_scr, 0)

    lam = _lambda(lq_ref, lk_ref, lam_init)
    o = acc_scr[0] / l_scr[0] - lam * (acc_scr[1] / l_scr[1])
    o = o * lax.rsqrt(jnp.mean(o * o, axis=0, keepdims=True) + RMS_EPS)
    o_ref[...] = (o.T * gs_ref[...] * (1.0 - lam_init)).astype(o_ref.dtype)


def _prompt_attention(qt, kb, vt, bias, lq, lk, g_sub, *, t, nd, lam_init):
    aw, s = qt.shape
    vmem = (2 * t * LANES * 2 + 2 * LANES * s * 2 + 2 * s * LANES * 2
            + 2 * (nd + 1) * t * t * 4 + 2 * t * LANES * 2
            + 2 * t * (2 * SUBLANES + LANES) * 4 + 4 * t * t * 4 + 8 * t * t * 4)
    return pl.pallas_call(
        functools.partial(_prompt_attn_kernel, t=t, nd=nd, lam_init=lam_init),
        grid=(N_HEADS, s // t),
        in_specs=[pl.BlockSpec((V_HEAD_DIM, t), lambda h, i: (h, i)),
                  pl.BlockSpec((s, V_HEAD_DIM), lambda h, i: (0, h)),
                  pl.BlockSpec((V_HEAD_DIM, s), lambda h, i: (h, 0)),
                  pl.BlockSpec((1, nd + 1, t, t), lambda h, i: (h, 0, 0, 0)),
                  pl.BlockSpec((2, QK_HEAD_DIM), lambda h, i: (0, 0)),
                  pl.BlockSpec((2, QK_HEAD_DIM), lambda h, i: (0, 0)),
                  pl.BlockSpec((1, V_HEAD_DIM), lambda h, i: (0, 0))],
        out_specs=pl.BlockSpec((t, V_HEAD_DIM), lambda h, i: (i, h)),
        out_shape=jax.ShapeDtypeStruct((s, aw), BF16),
        scratch_shapes=[pltpu.VMEM((2, 1, t), F32), pltpu.VMEM((2, 1, t), F32),
                        pltpu.VMEM((2, V_HEAD_DIM, t), F32),
                        pltpu.VMEM((2, t, t), F32), pltpu.VMEM((2, t, t), F32)],
        compiler_params=pltpu.CompilerParams(
            dimension_semantics=("arbitrary", "arbitrary"),
            vmem_limit_bytes=_vmem_limit(vmem)),
        name="prompt_attn",
    )(qt, kb, vt, bias, lq, lk, g_sub)


def _sample_attn_kernel(pt_ref, qr_ref, kn_ref, vn_ref, bias_ref, bself_ref, lq_ref, lk_ref,
                        gs_ref, *rest, pages, page, lam_init):
    del pt_ref
    k_refs = rest[:pages]
    v_refs = rest[pages:2 * pages]
    o_ref, m_scr, l_scr, acc_scr = rest[2 * pages:]
    step = pl.program_id(1)
    qr = qr_ref[0]
    half = qr.shape[0] // 2

    def both_maps(b):
        return jnp.concatenate([b, b], axis=0)

    def update(s, vs):
        m_prev = m_scr[...]
        m_new = jnp.maximum(m_prev, jnp.max(s, axis=-1, keepdims=True))
        p = jnp.exp2(s - m_new)
        alpha = jnp.exp2(m_prev - m_new)
        l_scr[...] = alpha * l_scr[...] + jnp.sum(p, axis=-1, keepdims=True)
        p = p.astype(BF16)
        acc = alpha * acc_scr[...]
        for j, vj in enumerate(vs):
            acc = acc + _dot(p[:, j * page:(j + 1) * page], vj)
        acc_scr[...] = acc
        m_scr[...] = m_new

    @pl.when(step == 0)
    def _():
        m_scr[...] = jnp.full(m_scr.shape, NEG_INF, F32)
        l_scr[...] = jnp.zeros(l_scr.shape, F32)
        acc_scr[...] = jnp.zeros(acc_scr.shape, F32)
        update(_dot(qr, kn_ref[0]) + both_maps(bself_ref[...]), [vn_ref[0]])

    def v_page(ref):
        return jnp.concatenate(
            [ref[0, pl.ds(h, page, stride=N_HEADS), :] for h in range(N_HEADS)],
            axis=1).astype(BF16)

    s = jnp.concatenate(
        [_dot(qr, k_refs[j][0].astype(BF16)) + both_maps(bias_ref[j]) for j in range(pages)],
        axis=1)
    update(s, [v_page(v_refs[j]) for j in range(pages)])

    @pl.when(step == pl.num_programs(1) - 1)
    def _():
        o = acc_scr[...] / l_scr[...]
        lam = _lambda(lq_ref, lk_ref, lam_init)
        o = o[:half] - lam * o[half:]
        row_head = lax.broadcasted_iota(jnp.int32, o.shape, 0) % N_HEADS
        lane_head = lax.broadcasted_iota(jnp.int32, o.shape, 1) // V_HEAD_DIM
        o = jnp.where(row_head == lane_head, o, 0.0)
        ms = jnp.sum(o * o, axis=-1, keepdims=True) * (1.0 / V_HEAD_DIM)
        o = o * lax.rsqrt(ms + RMS_EPS)
        o = jnp.sum(o.reshape(half // N_HEADS, N_HEADS, o.shape[1]), axis=1)
        o_ref[0] = (o * gs_ref[...] * (1.0 - lam_init)).astype(o_ref.dtype)


def _sample_attention(page_table, q_rows, kt_new, v_new, bias_pages, bias_self, lq, lk, g_sub_t,
                      cache_kt, cache_vr, *, pages, lam_init):
    nb, n_pages = page_table.shape
    _, aw, page = cache_kt.shape
    dt = q_rows.shape[1] // (2 * N_HEADS)
    steps = n_pages // pages
    assert n_pages % pages == 0
    rows = q_rows.shape[1]

    def page_spec(shape):
        return [pl.BlockSpec((1,) + shape,
                             lambda b, s, pt, j=j: (pt[b * n_pages + s * pages + j], 0, 0))
                for j in range(pages)]

    per_b = lambda shape: pl.BlockSpec(shape, lambda b, s, pt: (b, 0, 0))
    const2 = lambda shape: pl.BlockSpec(shape, lambda b, s, pt: (0, 0))
    vmem = (2 * 2 * pages * page * aw * 4 + 2 * pages * page * aw * 2
            + 4 * rows * aw * 4 + 2 * 3 * rows * aw * 2)
    grid_spec = pltpu.PrefetchScalarGridSpec(
        num_scalar_prefetch=1,
        grid=(nb, steps),
        in_specs=[per_b((1, rows, aw)), per_b((1, aw, page)), per_b((1, page, aw)),
                  pl.BlockSpec((pages, rows // 2, page), lambda b, s, pt: (s, 0, 0)),
                  const2((rows // 2, page)),
                  const2((2, QK_HEAD_DIM)), const2((2, QK_HEAD_DIM)), const2((1, aw))]
                 + page_spec((aw, page)) + page_spec((page * N_HEADS, V_HEAD_DIM)),
        out_specs=pl.BlockSpec((1, dt, aw), lambda b, s, pt: (b, 0, 0)),
        scratch_shapes=[pltpu.VMEM((rows, 1), F32), pltpu.VMEM((rows, 1), F32),
                        pltpu.VMEM((rows, aw), F32)])
    return pl.pallas_call(
        functools.partial(_sample_attn_kernel, pages=pages, page=page, lam_init=lam_init),
        grid_spec=grid_spec,
        out_shape=jax.ShapeDtypeStruct((nb, dt, aw), BF16),
        compiler_params=pltpu.CompilerParams(
            dimension_semantics=("arbitrary", "arbitrary"),
            vmem_limit_bytes=_vmem_limit(vmem)),
        name="sample_attn",
    )(page_table.reshape(-1), q_rows, kt_new, v_new, bias_pages, bias_self, lq, lk, g_sub_t,
      *([cache_kt] * pages), *([cache_vr] * pages))


def _out_proj_kernel(*refs, halo):
    if halo:
        a_ref, gb_ref, u_ref, prev_ref, cw_ref, wo_ref, x_ref, o_ref = refs
        u = u_ref[...]
        first = pl.program_id(0) == 0
        prev = jnp.where(first, 0.0, prev_ref[...])
        ucat = jnp.concatenate([prev, u], axis=0)
        p1 = pltpu.roll(ucat, 1, axis=0)[SUBLANES:]
        p2 = pltpu.roll(ucat, 2, axis=0)[SUBLANES:]
    else:
        a_ref, gb_ref, u_ref, p1_ref, p2_ref, cw_ref, wo_ref, x_ref, o_ref = refs
        u, p1, p2 = u_ref[...], p1_ref[...], p2_ref[...]
    cw = cw_ref[...]
    aw = u.shape[1]
    c = gb_ref[...] * (p2 * cw[0:1] + p1 * cw[1:2] + u * cw[2:3])
    y = _dot(a_ref[...], wo_ref[:aw, :]) + _dot(c.astype(BF16), wo_ref[aw:, :])
    o_ref[...] = x_ref[...] + y


def _out_project(a, gb, u, conv_w, wo_bf16, x, prev=None):
    rows, d = x.shape
    aw = a.shape[1]
    tm = min(rows, 256)
    assert rows % tm == 0
    row_blk = lambda width: pl.BlockSpec((tm, width), lambda i: (i, 0))
    const = lambda shape: pl.BlockSpec(shape, lambda i: (0, 0))
    halo = prev is None
    if halo:
        nsub = tm // SUBLANES
        extra = [pl.BlockSpec((SUBLANES, aw), lambda i: (jnp.maximum(i * nsub - 1, 0), 0))]
        extra_args = [u]
    else:
        extra = [row_blk(aw), row_blk(aw)]
        extra_args = list(prev)
    vmem = 2 * d * d * 2 + 2 * tm * (aw * (2 + 4 + 4 + 8) + d * 8) + 6 * tm * d * 4
    return pl.pallas_call(
        functools.partial(_out_proj_kernel, halo=halo),
        grid=(rows // tm,),
        in_specs=[row_blk(aw), row_blk(aw), row_blk(aw)] + extra
                 + [const((CONV_K, aw)), const((d, d)), row_blk(d)],
        out_specs=row_blk(d),
        out_shape=jax.ShapeDtypeStruct((rows, d), F32),
        compiler_params=pltpu.CompilerParams(
            dimension_semantics=("arbitrary",), vmem_limit_bytes=_vmem_limit(vmem)),
        name="out_proj",
    )(a, gb, u, *extra_args, conv_w, wo_bf16, x)


ROUTER_ROWS = 32
EXPERT_ROW0 = 8


def _router_kernel(x_ref, g_ref, w3_ref, bias_ref, idx_ref, wt_ref):
    x = x_ref[...]
    ms = jnp.mean(x * x, axis=-1, keepdims=True)
    xn = x * lax.rsqrt(ms + RMS_EPS) * g_ref[...]
    hi, lo = _split_bf16(xn)
    logits = _dot_nt(w3_ref[...], jnp.concatenate([hi, hi, lo], axis=1)) + bias_ref[...]
    big = jnp.int32(1 << 20)

    gl = logits[0:SUBLANES]
    grow = lax.broadcasted_iota(jnp.int32, gl.shape, 0)
    gmax = jnp.max(gl, axis=0, keepdims=True)
    g_idx = jnp.min(jnp.where(gl == gmax, grow, big), axis=0, keepdims=True)
    g_w = 1.0 / jnp.sum(jnp.exp(gl - gmax), axis=0, keepdims=True)

    el = logits[EXPERT_ROW0:EXPERT_ROW0 + N_EXPERTS]
    erow = lax.broadcasted_iota(jnp.int32, el.shape, 0)
    el = jnp.where(erow // EXPERTS_PER_GROUP == g_idx, el, NEG_INF)
    v1 = jnp.max(el, axis=0, keepdims=True)
    i1 = jnp.min(jnp.where(el == v1, erow, big), axis=0, keepdims=True)
    el2 = jnp.where(erow == i1, NEG_INF, el)
    v2 = jnp.max(el2, axis=0, keepdims=True)
    i2 = jnp.min(jnp.where(el2 == v2, erow, big), axis=0, keepdims=True)
    t = jnp.exp(v2 - v1)
    w1 = g_w / (1.0 + t)
    idx_ref[...] = jnp.concatenate([i1, i2], axis=0)
    wt_ref[...] = jnp.concatenate([w1, w1 * t], axis=0)


def _route(x, g, w3, bias_col):
    rows, d = x.shape
    tm = math.gcd(rows, 256)
    assert tm % LANES == 0
    return pl.pallas_call(
        _router_kernel,
        grid=(rows // tm,),
        in_specs=[pl.BlockSpec((tm, d), lambda i: (i, 0)),
                  pl.BlockSpec((1, d), lambda i: (0, 0)),
                  pl.BlockSpec((ROUTER_ROWS, 3 * d), lambda i: (0, 0)),
                  pl.BlockSpec((ROUTER_ROWS, 1), lambda i: (0, 0))],
        out_specs=[pl.BlockSpec((2, tm), lambda i: (0, i)),
                   pl.BlockSpec((2, tm), lambda i: (0, i))],
        out_shape=[jax.ShapeDtypeStruct((2, rows), jnp.int32),
                   jax.ShapeDtypeStruct((2, rows), F32)],
        compiler_params=pltpu.CompilerParams(dimension_semantics=("arbitrary",)),
        name="router",
    )(x, g, w3, bias_col)


EXPERT_PHASES = 4
EXPERT_TILE = 256


def _expert_kernel(te_ref, nv_ref, src_cur, src_next, dst_prev, dst_cur, roww_ref, g_ref,
                   wg_ref, wu_ref, wd_ref, x_hbm, y_hbm,
                   xbuf, ybuf, wgb, wub, wdb, gsem, ssem, *, tm):
    j = pl.program_id(0)
    last = nv_ref[0] - 1
    slot = j % 2
    other = 1 - slot

    def gather_rows(idx_ref, buf, lo, hi):
        for r in range(lo, hi):
            pltpu.make_async_copy(x_hbm.at[pl.ds(idx_ref[0, 0, r], 1)],
                                  xbuf.at[buf, pl.ds(r, 1)], gsem.at[buf]).start()

    def scatter_rows(idx_ref, buf, lo, hi):
        for r in range(lo, hi):
            pltpu.make_async_copy(ybuf.at[buf, pl.ds(r, 1)],
                                  y_hbm.at[pl.ds(idx_ref[0, 0, r], 1)], ssem.at[buf]).start()

    def wait_gather(buf):
        pltpu.make_async_copy(x_hbm.at[pl.ds(0, tm)], xbuf.at[buf], gsem.at[buf]).wait()

    def wait_scatter(buf):
        pltpu.make_async_copy(ybuf.at[buf], y_hbm.at[pl.ds(0, tm)], ssem.at[buf]).wait()

    @pl.when(j <= last)
    def _():
        @pl.when(j == 0)
        def _():
            ybuf[1] = jnp.zeros(ybuf.shape[1:], F32)
            gather_rows(src_cur, 0, 0, tm)

        @pl.when(jnp.logical_or(j == 0, te_ref[j] != te_ref[jnp.maximum(j - 1, 0)]))
        def _():
            wgb[...] = wg_ref[0].astype(BF16)
            wub[...] = wu_ref[0].astype(BF16)
            wdb[...] = wd_ref[0].astype(BF16)

        wait_gather(slot)

        @pl.when(j >= 1)
        def _():
            wait_scatter(slot)

        x = xbuf[slot]
        ms = jnp.mean(x * x, axis=-1, keepdims=True)
        xn = (x * lax.rsqrt(ms + RMS_EPS) * g_ref[...]).astype(BF16)
        roww = roww_ref[:, 0:1]
        resid = roww_ref[:, 1:2]
        ff = wgb.shape[1]
        d = wdb.shape[1]
        step = tm // EXPERT_PHASES
        half_ff, half_d = ff // 2, d // 2
        hs = []
        for c in range(2):
            gather_rows(src_next, other, (2 * c) * step, (2 * c + 2) * step)
            cols = slice(c * half_ff, (c + 1) * half_ff)
            hg = _dot(xn, wgb[:, cols])
            hu = _dot(xn, wub[:, cols])
            hs.append((hg * jax.nn.sigmoid(hg) * hu * roww).astype(BF16))
        h = jnp.concatenate(hs, axis=1)
        for c in range(2):
            scatter_rows(dst_prev, other, (2 * c) * step, (2 * c + 2) * step)
            cols = slice(c * half_d, (c + 1) * half_d)
            ybuf[slot, :, cols] = _dot(h, wdb[:, cols]) + x[:, cols] * resid

        @pl.when(j == last)
        def _():
            wait_scatter(other)
            scatter_rows(dst_cur, slot, 0, tm)
            wait_gather(other)
            wait_scatter(slot)


def _experts(tile_expert, n_used, src_cur, src_next, dst_prev, dst_cur, row_w, g, w_gate, w_up,
             w_down, x, *, tm, out_rows):
    nt = tile_expert.shape[0]
    d = x.shape[1]
    ff = w_gate.shape[2]
    assert tm % EXPERT_PHASES == 0
    smem_row = pl.BlockSpec((1, 1, tm), lambda i, te, nv: (i, 0, 0), memory_space=pltpu.SMEM)
    expert_w = lambda shape: pl.BlockSpec((1,) + shape, lambda i, te, nv: (te[i], 0, 0))
    grid_spec = pltpu.PrefetchScalarGridSpec(
        num_scalar_prefetch=2,
        grid=(nt,),
        in_specs=[smem_row, smem_row, smem_row, smem_row,
                  pl.BlockSpec((tm, 2), lambda i, te, nv: (i, 0)),
                  pl.BlockSpec((1, d), lambda i, te, nv: (0, 0)),
                  expert_w((d, ff)), expert_w((d, ff)), expert_w((ff, d)),
                  pl.BlockSpec(memory_space=pl.ANY)],
        out_specs=pl.BlockSpec(memory_space=pl.ANY),
        scratch_shapes=[pltpu.VMEM((2, tm, d), F32), pltpu.VMEM((2, tm, d), F32),
                        pltpu.VMEM((d, ff), BF16), pltpu.VMEM((d, ff), BF16),
                        pltpu.VMEM((ff, d), BF16),
                        pltpu.SemaphoreType.DMA((2,)), pltpu.SemaphoreType.DMA((2,))])
    vmem = 2 * 3 * d * ff * 4 + 3 * d * ff * 2 + 4 * tm * d * 4 + 6 * tm * d * 4
    return pl.pallas_call(
        functools.partial(_expert_kernel, tm=tm),
        grid_spec=grid_spec,
        out_shape=jax.ShapeDtypeStruct((out_rows, d), F32),
        compiler_params=pltpu.CompilerParams(
            dimension_semantics=("arbitrary",), vmem_limit_bytes=_vmem_limit(vmem)),
        name="experts",
    )(tile_expert, n_used, src_cur, src_next, dst_prev, dst_cur, row_w, g, w_gate, w_up, w_down,
      x)


def _combine_kernel(y0_ref, y1_ref, o_ref):
    o_ref[...] = y0_ref[...] + y1_ref[...]


def _combine(y2, row0, rows, total_rows):
    d = y2.shape[1]
    tm = math.gcd(math.gcd(rows, row0) if row0 else rows, LANES)
    assert total_rows % tm == 0 and tm % SUBLANES == 0
    b0 = row0 // tm
    b1 = (total_rows + row0) // tm
    return pl.pallas_call(
        _combine_kernel,
        grid=(rows // tm,),
        in_specs=[pl.BlockSpec((tm, d), lambda i: (b0 + i, 0)),
                  pl.BlockSpec((tm, d), lambda i: (b1 + i, 0))],
        out_specs=pl.BlockSpec((tm, d), lambda i: (i, 0)),
        out_shape=jax.ShapeDtypeStruct((rows, d), F32),
        compiler_params=pltpu.CompilerParams(dimension_semantics=("arbitrary",)),
        name="moe_combine",
    )(y2, y2)


def _routing_plan(e_idx, w, tm):
    total = e_idx.shape[1]
    n_assign = 2 * total
    nt = -(-n_assign // tm) + N_EXPERTS
    e_flat = e_idx.reshape(-1)
    w_flat = w.reshape(-1)
    order = jnp.argsort(e_flat, stable=True).astype(jnp.int32)
    counts = jnp.sum((e_flat[:, None] == jnp.arange(N_EXPERTS, dtype=jnp.int32)[None, :])
                     .astype(jnp.int32), axis=0)
    tiles_per = (counts + tm - 1) // tm
    tile_end = jnp.cumsum(tiles_per)
    tile_start = tile_end - tiles_per
    group_start = jnp.cumsum(counts) - counts
    tiles = jnp.arange(nt, dtype=jnp.int32)
    tile_expert = jnp.minimum(
        jnp.sum((tiles[:, None] >= tile_end[None, :]).astype(jnp.int32), axis=1), N_EXPERTS - 1)
    tile_valid = (tiles < tile_end[-1]).astype(jnp.int32)
    rank = (tiles - tile_start[tile_expert])[:, None] * tm + jnp.arange(tm, dtype=jnp.int32)[None, :]
    valid = (rank < counts[tile_expert][:, None]) & (tile_valid[:, None] > 0)
    pos = jnp.clip(group_start[tile_expert][:, None] + rank, 0, n_assign - 1)
    assign = order[pos]
    r_in_tile = jnp.broadcast_to(jnp.arange(tm, dtype=jnp.int32)[None, :], (nt, tm))
    row_src = jnp.where(valid, assign % total, 0)
    row_dst = jnp.where(valid, assign, n_assign + r_in_tile)
    row_w = jnp.stack([jnp.where(valid, w_flat[assign], 0.0),
                       (valid & (assign < total)).astype(F32)], axis=-1)
    src_next = jnp.concatenate([row_src[1:], row_src[-1:]], axis=0)
    dst_prev = jnp.concatenate([n_assign + r_in_tile[:1], row_dst[:-1]], axis=0)
    as_smem = lambda a: a.reshape(nt, 1, tm)
    n_used = tile_end[-1:].astype(jnp.int32)
    return (tile_expert, n_used, as_smem(row_src), as_smem(src_next), as_smem(dst_prev),
            as_smem(row_dst), row_w.reshape(nt * tm, 2)), n_assign + tm


def _group_matrices(aw):
    e = jnp.arange(aw, dtype=jnp.int32) // QK_HEAD_DIM
    lanes = jnp.arange(LANES, dtype=jnp.int32)
    member = (e[:, None] == lanes[None, :]).astype(BF16)
    return jnp.concatenate([member, member], axis=0), jnp.concatenate([member.T, member.T], axis=0)


def _toeplitz_kernel(strip_ref, out_ref, *, t):
    x = jnp.broadcast_to(strip_ref[0], (t, 2 * t))
    out_ref[0, 0] = pltpu.roll(x, t + 1, axis=1, stride=1, stride_axis=0)[:, :t]


def _prompt_bias(rel_table, t, nd):
    m = jnp.arange(2 * t, dtype=jnp.int32)
    dist = jnp.arange(nd + 1, dtype=jnp.int32)[:, None] * t - (t - 1) + m[None, :]
    strips = _bias_table(rel_table, dist.reshape(-1, LANES))
    strips = strips.reshape(N_HEADS * (nd + 1), 1, 2 * t)
    return pl.pallas_call(
        functools.partial(_toeplitz_kernel, t=t),
        grid=(N_HEADS, nd + 1),
        in_specs=[pl.BlockSpec((1, 1, 2 * t), lambda h, dl: (h * (nd + 1) + dl, 0, 0))],
        out_specs=pl.BlockSpec((1, 1, t, t), lambda h, dl: (h, dl, 0, 0)),
        out_shape=jax.ShapeDtypeStruct((N_HEADS, nd + 1, t, t), F32),
        name="bias_toeplitz",
    )(strips)


def _sample_bias(rel_table, past, n_pages, page, dt):
    tq = jnp.arange(dt, dtype=jnp.int32)
    kpos = jnp.arange(n_pages * page, dtype=jnp.int32).reshape(n_pages, 1, page)
    dist_pages = (past + tq[None, :, None] - kpos).reshape(-1, page)
    j = jnp.arange(page, dtype=jnp.int32)
    self_rows = jnp.where(j[None, :] < dt, tq[:, None] - j[None, :], -1)
    dist = jnp.concatenate([dist_pages, self_rows], axis=0)
    rows = dist.shape[0]
    rpad = (-rows) % (256 if rows > 256 else SUBLANES)
    dist = jnp.concatenate([dist, jnp.full((rpad, page), -1, jnp.int32)], axis=0)
    bias = _bias_table(rel_table, dist)
    bp = bias[:, :n_pages * dt].reshape(N_HEADS, n_pages, dt, page)
    bp = bp.transpose(1, 2, 0, 3).reshape(n_pages, dt * N_HEADS, page)
    bs = bias[:, n_pages * dt:n_pages * dt + dt].transpose(1, 0, 2).reshape(dt * N_HEADS, page)
    return bp, bs


def kernel(x_prompt, x_sample, cache_k, cache_v, state_conv, page_table, rel_table, attn_norm_g,
           w_in, q_norm_g, k_norm_g, lambda_q, lambda_k, sub_norm_g, conv_w, w_o, ffn_norm_g,
           router_group_w, router_group_b, router_expert_w, router_expert_b, w_gate, w_up, w_down):
    depth = w_in.shape[0]
    assert depth == 1, "single-layer trunk"
    batch, seq, d = x_prompt.shape
    assert batch == 1, "one prompt sequence"
    nb, dt, _ = x_sample.shape
    n_pool, page = cache_k.shape[1], cache_k.shape[2]
    n_pages = page_table.shape[1]
    past = n_pages * page
    aw = N_HEADS * V_HEAD_DIM
    layer = 0
    lam_init = 0.8 - 0.6 * math.exp(-0.3 * layer)

    w_in_b = w_in[layer].astype(BF16)
    w_o_b = w_o[layer].astype(BF16)
    g_attn = attn_norm_g[layer].reshape(1, d)
    g_ffn = ffn_norm_g[layer].reshape(1, d)
    qg = jnp.tile(q_norm_g[layer], aw // QK_HEAD_DIM).reshape(1, aw)
    kg = jnp.tile(k_norm_g[layer], aw // QK_HEAD_DIM).reshape(1, aw)
    g_sub = sub_norm_g[layer].reshape(1, V_HEAD_DIM)
    g_sub_t = jnp.tile(sub_norm_g[layer], N_HEADS).reshape(1, aw)
    gsum, gexp = _group_matrices(aw)
    lq, lk = lambda_q[layer], lambda_k[layer]
    cw = conv_w[layer]

    xp = x_prompt.reshape(seq, d)
    xs = x_sample.reshape(nb * dt, d)

    qtp, kp, kbp, vp, vtp, gbp, up = _project(xp, g_attn, w_in_b, qg, kg, gsum, gexp)
    qts, ks, _, vs, _, gbs, us = _project(xs, g_attn, w_in_b, qg, kg, gsum, gexp)

    t = min(seq, 512)
    assert seq % t == 0
    nd = min(-(-(MAX_DISTANCE + t - 1) // t), seq // t)
    bias_p = _prompt_bias(rel_table, t, nd)
    ap = _prompt_attention(qtp, kbp, vtp, bias_p, lq, lk, g_sub, t=t, nd=nd, lam_init=lam_init)

    bias_pages, bias_self = _sample_bias(rel_table, past, n_pages, page, dt)
    lane_hc = jnp.arange(aw, dtype=jnp.int32) // QK_HEAD_DIM
    row_c = jnp.arange(2, dtype=jnp.int32)[:, None, None]
    row_h = jnp.arange(N_HEADS, dtype=jnp.int32)[None, None, :]
    sel = (lane_hc[None, None, None, :] == (row_h * 2 + row_c)[..., None])
    q3 = qts.T.reshape(nb, 1, dt, 1, aw)
    q_rows = jnp.where(sel[None], q3, jnp.zeros((), BF16)).reshape(nb, 2 * dt * N_HEADS, aw)
    k3 = ks.reshape(nb, dt, aw).astype(BF16)
    kt_new = jnp.pad(k3.transpose(0, 2, 1), ((0, 0), (0, 0), (0, page - dt)))
    v_new = jnp.pad(vs.reshape(nb, dt, aw).astype(BF16), ((0, 0), (0, page - dt), (0, 0)))
    cache_kt = cache_k[layer].transpose(0, 2, 3, 4, 1).reshape(n_pool, aw, page)
    cache_vr = cache_v[layer].reshape(n_pool, page * N_HEADS, V_HEAD_DIM)
    pages_per_step = math.gcd(n_pages, PAGES_PER_STEP)
    a_s = _sample_attention(page_table, q_rows, kt_new, v_new, bias_pages, bias_self,
                            lq, lk, g_sub_t, cache_kt, cache_vr,
                            pages=pages_per_step, lam_init=lam_init)
    a_s = a_s.reshape(nb * dt, aw)

    x2p = _out_project(ap, gbp, up, cw, w_o_b, xp)
    u_ext = jnp.concatenate([state_conv[layer].astype(F32), us.reshape(nb, dt, aw)], axis=1)
    prev1 = u_ext[:, 1:1 + dt].reshape(nb * dt, aw)
    prev2 = u_ext[:, 0:dt].reshape(nb * dt, aw)
    x2s = _out_project(a_s, gbs, us, cw, w_o_b, xs, prev=(prev1, prev2))
    x2 = jnp.concatenate([x2p, x2s], axis=0)
    total = seq + nb * dt

    wr = jnp.zeros((ROUTER_ROWS, d), F32)
    wr = wr.at[0:N_GROUPS].set(router_group_w[layer].T)
    wr = wr.at[EXPERT_ROW0:EXPERT_ROW0 + N_EXPERTS].set(router_expert_w[layer].T)
    wr_hi, wr_lo = _split_bf16(wr)
    w3 = jnp.concatenate([wr_hi, wr_lo, wr_hi], axis=1)
    bias_col = jnp.full((ROUTER_ROWS,), NEG_INF, F32)
    bias_col = bias_col.at[0:N_GROUPS].set(router_group_b[layer].astype(F32))
    bias_col = bias_col.at[EXPERT_ROW0:EXPERT_ROW0 + N_EXPERTS].set(
        router_expert_b[layer].reshape(-1).astype(F32)).reshape(ROUTER_ROWS, 1)
    e_idx, wts = _route(x2, g_ffn, w3, bias_col)
    tm_e = EXPERT_TILE
    plan, y_rows = _routing_plan(e_idx, wts, tm_e)
    y2 = _experts(*plan, g_ffn, w_gate[layer], w_up[layer], w_down[layer], x2,
                  tm=tm_e, out_rows=y_rows)
    yp = _combine(y2, 0, seq, total)
    ys = _combine(y2, seq, nb * dt, total)

    return (yp.reshape(batch, seq, d),
            ys.reshape(nb, dt, d),
            kp.reshape(depth, batch, seq, N_HEADS, 2, QK_HEAD_DIM),
            vp.reshape(depth, batch, seq, N_HEADS, V_HEAD_DIM),
            up[seq - (CONV_K - 1):].reshape(depth, batch, CONV_K - 1, aw),
            ks.reshape(depth, nb, dt, N_HEADS, 2, QK_HEAD_DIM),
            vs.reshape(depth, nb, dt, N_HEADS, V_HEAD_DIM),
            u_ext[:, dt:].reshape(depth, nb, CONV_K - 1, aw))
```
